```python
import jax, jax.numpy as jnp
from jax import lax
import numpy as np

D_MODEL = 2048
BATCH = 1
SEQ = 16384
DEPTH = 2

MIX_WIDTH = 512
HEAD_DIM = 64
N_BRANCH = 4
POOL_GROUPS = 4
POOL_CH = MIX_WIDTH // POOL_GROUPS
POOL_WINDOWS = (2, 4, 8, 16)
SWA_HEADS = 8
SWA_KV_HEADS = 2
SWA_WINDOW = 128
SWA_BLOCK = 128
ROPE_THETA = 10000.0
MLSTM_HEADS = 4
MLSTM_HEAD_DIM = 128
MLSTM_CHUNK = 64
MLSTM_CONV = 4
FOX_HEADS = 8
FOX_BLOCK = 128
D_FF = 5632
FFN_HALF = 0.5
LN_EPS = 1e-5
DEEPNORM_ALPHA = (2 * DEPTH) ** 0.25
DEEPNORM_BETA = (8 * DEPTH) ** -0.25

SWA_Q = SWA_HEADS * HEAD_DIM
SWA_KV = SWA_KV_HEADS * HEAD_DIM
ML_W = MLSTM_HEADS * MLSTM_HEAD_DIM
FOX_W = FOX_HEADS * HEAD_DIM
IN_SPLITS = (MIX_WIDTH,
             SWA_Q, SWA_KV, SWA_KV,
             ML_W, ML_W, ML_W, ML_W,
             MLSTM_HEADS, MLSTM_HEADS,
             FOX_W, FOX_W, FOX_W, FOX_HEADS)
D_IN = 4880

kernel_name = 'hybrid_pool_swa_mlstm_fox_deepnorm'

F32 = jnp.float32


def layer_norm(x, g, b):
    xf = x.astype(F32)
    mu = jnp.mean(xf, axis=-1, keepdims=True)
    var = jnp.mean(jnp.square(xf - mu), axis=-1, keepdims=True)
    return ((xf - mu) * lax.rsqrt(var + LN_EPS) * g.astype(F32) + b.astype(F32)).astype(x.dtype)


def swiglu(x, w_in, w_out):
    gate, up = jnp.split(x @ w_in, 2, axis=-1)
    return (jax.nn.silu(gate) * up) @ w_out


def split_columns(z):
    parts, off = [], 0
    for size in IN_SPLITS:
        parts.append(z[..., off:off + size])
        off += size
    return parts


def rope(x, pos):
    half = x.shape[-1] // 2
    inv = ROPE_THETA ** (-jnp.arange(half, dtype=F32) / half)
    ang = pos.astype(F32)[:, None] * inv[None, :]
    cos = jnp.cos(ang)[None, :, None, :]
    sin = jnp.sin(ang)[None, :, None, :]
    xf = x.astype(F32)
    x1, x2 = xf[..., :half], xf[..., half:]
    return jnp.concatenate([x1 * cos - x2 * sin, x2 * cos + x1 * sin], axis=-1)


def pool_mixer(u, w_pool, scale):
    B, S, _ = u.shape
    uf = u.astype(F32).reshape(B, S, POOL_GROUPS, POOL_CH)
    cs = jnp.concatenate([jnp.zeros_like(uf[:, :1]), jnp.cumsum(uf, axis=1)], axis=1)
    t1 = jnp.arange(1, S + 1, dtype=jnp.int32)[:, None]
    win = jnp.array(POOL_WINDOWS, dtype=jnp.int32)[None, :]
    lo = jnp.maximum(t1 - win, 0)
    lo_val = cs[:, lo, jnp.arange(POOL_GROUPS)[None, :], :]
    count = (t1 - lo).astype(F32)[None, :, :, None]
    pooled = (cs[:, 1:] - lo_val) / count - uf
    y = jnp.einsum('bsgc,gcd->bsgd', pooled, w_pool.astype(F32))
    return (y.reshape(B, S, MIX_WIDTH) * scale.astype(F32)).astype(u.dtype)


def swa_attention(q, k, v, sinks, pos):
    B, S, _ = q.shape
    NB = S // SWA_BLOCK
    G = SWA_HEADS // SWA_KV_HEADS
    qr = rope(q.reshape(B, S, SWA_HEADS, HEAD_DIM), pos) * HEAD_DIM ** -0.5
    kr = rope(k.reshape(B, S, SWA_KV_HEADS, HEAD_DIM), pos)
    vr = v.astype(F32).reshape(B, S, SWA_KV_HEADS, HEAD_DIM)
    qb = qr.reshape(B, NB, SWA_BLOCK, SWA_KV_HEADS, G, HEAD_DIM)

    def band(t):
        tb = t.reshape(B, NB, SWA_BLOCK, SWA_KV_HEADS, HEAD_DIM)
        prev = jnp.concatenate([jnp.zeros_like(tb[:, :1]), tb[:, :-1]], axis=1)
        return jnp.concatenate([prev, tb], axis=2)

    kb, vb = band(kr), band(vr)
    s = jnp.einsum('bnqhgd,bnkhd->bnhgqk', qb, kb)
    tq = jnp.arange(SWA_BLOCK)[:, None]
    sk = jnp.arange(2 * SWA_BLOCK)[None, :]
    rel = SWA_BLOCK + tq - sk
    in_win = (rel >= 0) & (rel < SWA_WINDOW)
    blk = jnp.arange(NB)[:, None, None]
    valid = in_win[None] & ((blk * SWA_BLOCK - SWA_BLOCK + sk[None]) >= 0)
    s = jnp.where(valid[None, :, None, None], s, -jnp.inf)
    sink = sinks.astype(F32).reshape(1, 1, SWA_KV_HEADS, G, 1, 1)
    m = jnp.maximum(jnp.max(s, axis=-1, keepdims=True), sink)
    p = jnp.exp(s - m)
    den = jnp.sum(p, axis=-1, keepdims=True) + jnp.exp(sink - m)
    o = jnp.einsum('bnhgqk,bnkhd->bnqhgd', p / den, vb)
    return o.reshape(B, S, SWA_Q).astype(q.dtype)


def causal_depthwise_conv(x, w):
    K, C = w.shape
    return lax.conv_general_dilated(x, w[:, None, :].astype(x.dtype), window_strides=(1,),
                                    padding=[(K - 1, 0)],
                                    dimension_numbers=('NWC', 'WIO', 'NWC'),
                                    feature_group_count=C)


def mlstm(q, k, v, o_pre, i_pre, f_pre, conv_w, i_bias, f_bias):
    B, S, _ = q.shape
    H, dh, L = MLSTM_HEADS, MLSTM_HEAD_DIM, MLSTM_CHUNK
    NC = S // L
    qk = jax.nn.silu(causal_depthwise_conv(jnp.concatenate([q, k], axis=-1), conv_w))
    qc_in, kc_in = jnp.split(qk, 2, axis=-1)

    def chunks(t, d):
        return t.astype(F32).reshape(B, NC, L, H, d).transpose(1, 0, 3, 2, 4)

    qc = chunks(qc_in, dh) * dh ** -0.5
    kc = chunks(kc_in, dh)
    vc = chunks(v, dh)
    ig = chunks(i_pre.astype(F32) + i_bias.astype(F32), 1)[..., 0]
    lf = jax.nn.log_sigmoid(chunks(f_pre.astype(F32) + f_bias.astype(F32), 1)[..., 0])
    causal = jnp.tril(jnp.ones((L, L), dtype=bool))

    def step(carry, inp):
        C, n, m = carry
        qx, kx, vx, ix, fx = inp
        b = jnp.cumsum(fx, axis=-1)
        dlog = jnp.where(causal, b[..., :, None] - b[..., None, :] + ix[..., None, :], -jnp.inf)
        inter = b + m[..., None]
        m_comb = jnp.maximum(inter, jnp.max(dlog, axis=-1))
        sm = jnp.einsum('bhtd,bhsd->bhts', qx, kx) * jnp.exp(dlog - m_comb[..., None])
        w_inter = jnp.exp(inter - m_comb)
        num = jnp.einsum('bhts,bhsd->bhtd', sm, vx) + w_inter[..., None] * jnp.einsum('bhvk,bhtk->bhtv', C, qx)
        den = jnp.sum(sm, axis=-1) + w_inter * jnp.einsum('bhk,bhtk->bht', n, qx)
        h = num / jnp.maximum(jnp.abs(den), jnp.exp(-m_comb))[..., None]
        bl = b[..., -1]
        g = bl[..., None] - b + ix
        m_new = jnp.maximum(bl + m, jnp.max(g, axis=-1))
        wk = jnp.exp(g - m_new[..., None])
        decay = jnp.exp(bl + m - m_new)
        C = decay[..., None, None] * C + jnp.einsum('bhs,bhsv,bhsk->bhvk', wk, vx, kx)
        n = decay[..., None] * n + jnp.einsum('bhs,bhsk->bhk', wk, kx)
        return (C, n, m_new), h

    init = (jnp.zeros((B, H, dh, dh), F32), jnp.zeros((B, H, dh), F32), jnp.zeros((B, H), F32))
    _, h = lax.scan(step, init, (qc, kc, vc, ig, lf))
    h = h.transpose(1, 0, 3, 2, 4).reshape(B, S, ML_W)
    return (jax.nn.sigmoid(o_pre.astype(F32)) * h).astype(q.dtype)


def forgetting_attention(q, k, v, f_pre, f_bias):
    B, S, _ = q.shape
    H, d, Qb = FOX_HEADS, HEAD_DIM, FOX_BLOCK
    NB = S // Qb
    qh = q.astype(F32).reshape(B, S, H, d).transpose(0, 2, 1, 3) * d ** -0.5
    kh = k.astype(F32).reshape(B, S, H, d).transpose(0, 2, 1, 3)
    vh = v.astype(F32).reshape(B, S, H, d).transpose(0, 2, 1, 3)
    c = jnp.cumsum(jax.nn.log_sigmoid(f_pre.astype(F32) + f_bias.astype(F32)), axis=1).transpose(0, 2, 1)
    q_blocks = qh.reshape(B, H, NB, Qb, d).transpose(2, 0, 1, 3, 4)
    c_blocks = c.reshape(B, H, NB, Qb).transpose(2, 0, 1, 3)
    key_pos = jnp.arange(S)

    def block(args):
        qb, cb, i = args
        s = jnp.einsum('bhqd,bhkd->bhqk', qb, kh) + cb[..., None] - c[:, :, None, :]
        q_pos = i * Qb + jnp.arange(Qb)
        s = jnp.where(key_pos[None, :] <= q_pos[:, None], s, -jnp.inf)
        return jnp.einsum('bhqk,bhkd->bhqd', jax.nn.softmax(s, axis=-1), vh)

    o = lax.map(block, (q_blocks, c_blocks, jnp.arange(NB)))
    return o.transpose(1, 0, 3, 2, 4).reshape(B, S, FOX_W).astype(q.dtype)


def hybrid_mixer(x, pos, w_in, pool_w, pool_scale, swa_sinks, mlstm_conv,
                 mlstm_i_bias, mlstm_f_bias, fox_f_bias, w_branch, w_gate, w_out):
    z = x @ w_in
    (u_pool, q_swa, k_swa, v_swa, q_ml, k_ml, v_ml, o_ml, i_ml, f_ml,
     q_fox, k_fox, v_fox, f_fox) = split_columns(z)
    branches = (
        pool_mixer(u_pool, pool_w, pool_scale),
        swa_attention(q_swa, k_swa, v_swa, swa_sinks, pos),
        mlstm(q_ml, k_ml, v_ml, o_ml, i_ml, f_ml, mlstm_conv, mlstm_i_bias, mlstm_f_bias),
        forgetting_attention(q_fox, k_fox, v_fox, f_fox, fox_f_bias),
    )
    merged = jnp.zeros_like(x)
    for b in range(N_BRANCH):
        merged = merged + jax.nn.sigmoid(x @ w_gate[b]) * (branches[b] @ w_branch[b])
    return merged @ w_out


def setup_inputs(seed: int = 0) -> dict:
    key = jax.random.key(seed)
    ks = jax.random.split(key, 20)

    def nrm(k, shape, scale):
        return jax.random.normal(k, shape, F32) * scale

    x = nrm(ks[0], (BATCH, SEQ, D_MODEL), 1.0)
    ln_g = 1.0 + nrm(ks[1], (DEPTH, 3, D_MODEL), 0.02)
    ln_b = nrm(ks[2], (DEPTH, 3, D_MODEL), 0.02)
    ffn1_w_in = nrm(ks[3], (DEPTH, D_MODEL, 2 * D_FF), D_MODEL ** -0.5)
    ffn1_w_out = nrm(ks[4], (DEPTH, D_FF, D_MODEL), DEEPNORM_BETA * D_FF ** -0.5)
    w_in = nrm(ks[5], (DEPTH, D_MODEL, D_IN), D_MODEL ** -0.5)
    pool_w = nrm(ks[6], (DEPTH, POOL_GROUPS, POOL_CH, POOL_CH), POOL_CH ** -0.5)
    pool_scale = 1.0 + nrm(ks[7], (DEPTH, MIX_WIDTH), 0.02)
    swa_sinks = nrm(ks[8], (DEPTH, SWA_HEADS), 0.5)
    mlstm_conv = nrm(ks[9], (DEPTH, MLSTM_CONV, 2 * ML_W), MLSTM_CONV ** -0.5)
    mlstm_i_bias = nrm(ks[10], (DEPTH, MLSTM_HEADS), 0.1)
    mlstm_f_bias = jnp.linspace(3.0, 6.0, MLSTM_HEADS, dtype=F32)[None, :] + nrm(ks[11], (DEPTH, MLSTM_HEADS), 0.1)
    fox_f_bias = jnp.linspace(2.0, 6.0, FOX_HEADS, dtype=F32)[None, :] + nrm(ks[12], (DEPTH, FOX_HEADS), 0.1)
    w_branch = nrm(ks[13], (DEPTH, N_BRANCH, MIX_WIDTH, D_MODEL), MIX_WIDTH ** -0.5)
    w_gate = nrm(ks[14], (DEPTH, N_BRANCH, D_MODEL, D_MODEL), D_MODEL ** -0.5)
    w_out = nrm(ks[15], (DEPTH, D_MODEL, D_MODEL), DEEPNORM_BETA * D_MODEL ** -0.5)
    ffn2_w_in = nrm(ks[16], (DEPTH, D_MODEL, 2 * D_FF), D_MODEL ** -0.5)
    ffn2_w_out = nrm(ks[17], (DEPTH, D_FF, D_MODEL), DEEPNORM_BETA * D_FF ** -0.5)
    return {'x': x, 'ln_g': ln_g, 'ln_b': ln_b, 'ffn1_w_in': ffn1_w_in, 'ffn1_w_out': ffn1_w_out,
            'w_in': w_in, 'pool_w': pool_w, 'pool_scale': pool_scale, 'swa_sinks': swa_sinks,
            'mlstm_conv': mlstm_conv, 'mlstm_i_bias': mlstm_i_bias, 'mlstm_f_bias': mlstm_f_bias,
            'fox_f_bias': fox_f_bias, 'w_branch': w_branch, 'w_gate': w_gate, 'w_out': w_out,
            'ffn2_w_in': ffn2_w_in, 'ffn2_w_out': ffn2_w_out}


def reference(x, ln_g, ln_b, ffn1_w_in, ffn1_w_out, w_in, pool_w, pool_scale, swa_sinks,
              mlstm_conv, mlstm_i_bias, mlstm_f_bias, fox_f_bias, w_branch, w_gate, w_out,
              ffn2_w_in, ffn2_w_out):
    pos = jnp.arange(x.shape[1], dtype=jnp.int32)
    h = x
    for l in range(DEPTH):
        h = layer_norm(DEEPNORM_ALPHA * h + FFN_HALF * swiglu(h, ffn1_w_in[l], ffn1_w_out[l]),
                       ln_g[l, 0], ln_b[l, 0])
        mix = hybrid_mixer(h, pos, w_in[l], pool_w[l], pool_scale[l], swa_sinks[l], mlstm_conv[l],
                           mlstm_i_bias[l], mlstm_f_bias[l], fox_f_bias[l], w_branch[l],
                           w_gate[l], w_out[l])
        h = layer_norm(DEEPNORM_ALPHA * h + mix, ln_g[l, 1], ln_b[l, 1])
        h = layer_norm(DEEPNORM_ALPHA * h + FFN_HALF * swiglu(h, ffn2_w_in[l], ffn2_w_out[l]),
                       ln_g[l, 2], ln_b[l, 2])
    return h
```

```python
import functools

import jax
import jax.numpy as jnp
import numpy as np
from jax import lax
from jax.experimental import pallas as pl
from jax.experimental.pallas import tpu as pltpu

F32 = jnp.float32
BF16 = jnp.bfloat16

MIX_WIDTH = 512
HEAD_DIM = 64
POOL_WINDOWS = (2, 4, 8, 16)
POOL_CH = 128
SWA_HEADS = 8
SWA_KV_HEADS = 2
SWA_BLOCK = 128
ROPE_THETA = 10000.0
MLSTM_HEADS = 4
MLSTM_HEAD_DIM = 128
MLSTM_CONV = 4
FOX_HEADS = 8
LN_EPS = 1e-5
FFN_HALF = 0.5

LANES = 128
VMEM_LIMIT = 56 * 1024 * 1024

COL_POOL = 0
COL_SWA_Q = 512
COL_ML_Q = 1024
COL_ML_K = 1536
COL_ML_V = 2048
COL_ML_O = 2560
COL_FOX_Q = 3072
COL_FOX_K = 3584
COL_FOX_V = 4096
COL_SWA_K = 4608
COL_SWA_V = 4736
COL_GATES = 4864
Z_COLS = 4992
N_GATES = 16

ML_CHUNK = 256


def _cparams(sem):
    return pltpu.CompilerParams(dimension_semantics=sem, vmem_limit_bytes=VMEM_LIMIT)


def _layer_norm(y, g, b):
    mu = jnp.mean(y, axis=-1, keepdims=True)
    d = y - mu
    var = jnp.mean(d * d, axis=-1, keepdims=True)
    return d * lax.rsqrt(var + LN_EPS) * g + b


def _dot(a, b):
    return jnp.dot(a, b, preferred_element_type=F32)


def _dot_nt(a, b):
    return lax.dot_general(a, b, (((1,), (1,)), ((), ())), preferred_element_type=F32)


def _dot_tn(a, b):
    return lax.dot_general(a, b, (((0,), (0,)), ((), ())), preferred_element_type=F32)


def _ffn_kernel(x_ref, wg_ref, wu_ref, wo_ref, g_ref, b_ref, o_ref, ob_ref, xb_ref, acc_ref, *, alpha, nf):
    f = pl.program_id(1)

    @pl.when(f == 0)
    def _():
        xb_ref[...] = x_ref[...].astype(BF16)
        acc_ref[...] = jnp.zeros_like(acc_ref)

    xb = xb_ref[...]
    gate = _dot(xb, wg_ref[...])
    up = _dot(xb, wu_ref[...])
    act = (gate * jax.nn.sigmoid(gate) * up).astype(BF16)
    acc_ref[...] += _dot(act, wo_ref[...])

    @pl.when(f == nf - 1)
    def _():
        y = alpha * x_ref[...] + FFN_HALF * acc_ref[...]
        o = _layer_norm(y, g_ref[...], b_ref[...])
        o_ref[...] = o
        ob_ref[...] = o.astype(BF16)


def _ffn(x, w_in, w_out, g, b, *, alpha, tm, tf):
    S, D = x.shape
    DF = w_out.shape[0]
    nf = DF // tf
    assert S % tm == 0 and DF % tf == 0
    return pl.pallas_call(
        functools.partial(_ffn_kernel, alpha=alpha, nf=nf),
        grid=(S // tm, nf),
        in_specs=[
            pl.BlockSpec((tm, D), lambda i, f: (i, 0)),
            pl.BlockSpec((D, tf), lambda i, f: (0, f)),
            pl.BlockSpec((D, tf), lambda i, f: (0, f + nf)),
            pl.BlockSpec((tf, D), lambda i, f: (f, 0)),
            pl.BlockSpec((1, D), lambda i, f: (0, 0)),
            pl.BlockSpec((1, D), lambda i, f: (0, 0)),
        ],
        out_specs=[pl.BlockSpec((tm, D), lambda i, f: (i, 0)), pl.BlockSpec((tm, D), lambda i, f: (i, 0))],
        out_shape=[jax.ShapeDtypeStruct((S, D), F32), jax.ShapeDtypeStruct((S, D), BF16)],
        scratch_shapes=[pltpu.VMEM((tm, D), BF16), pltpu.VMEM((tm, D), F32)],
        compiler_params=_cparams(("parallel", "arbitrary")),
        name="ffn",
    )(x, w_in, w_in, w_out, g.reshape(1, D), b.reshape(1, D))


def _inproj_kernel(x_ref, w_ref, o_ref):
    o_ref[...] = _dot(x_ref[...], w_ref[...])


def _inproj(xb, w, *, tm, tn):
    S, D = xb.shape
    N = w.shape[1]
    assert S % tm == 0 and N % tn == 0
    return pl.pallas_call(
        _inproj_kernel,
        grid=(S // tm, N // tn),
        in_specs=[pl.BlockSpec((tm, D), lambda i, j: (i, 0)), pl.BlockSpec((D, tn), lambda i, j: (0, j))],
        out_specs=pl.BlockSpec((tm, tn), lambda i, j: (i, j)),
        out_shape=jax.ShapeDtypeStruct((S, N), F32),
        compiler_params=_cparams(("parallel", "arbitrary")),
        name="inproj",
    )(xb, w)


def _split3(x):
    hi = x.astype(BF16)
    r1 = x - hi.astype(F32)
    mid = r1.astype(BF16)
    lo = (r1 - mid.astype(F32)).astype(BF16)
    return hi, mid, lo


def _gates_kernel(z_ref, bias_ref, o_ref, carry_ref):
    c = pl.program_id(0)
    T = z_ref.shape[0]

    @pl.when(c == 0)
    def _():
        carry_ref[...] = jnp.zeros_like(carry_ref)

    pre = z_ref[:, 0:N_GATES] + bias_ref[...]
    lf = jax.nn.log_sigmoid(pre)
    row = lax.broadcasted_iota(jnp.int32, (T, T), 0)
    col = lax.broadcasted_iota(jnp.int32, (T, T), 1)
    tri = jnp.where(col <= row, 1.0, 0.0).astype(BF16)
    hi, mid, lo = _split3(lf)
    cs = _dot(tri, hi) + _dot(tri, mid) + _dot(tri, lo)
    lane = lax.broadcasted_iota(jnp.int32, (T, N_GATES), 1)
    is_fox = lane >= 2 * MLSTM_HEADS
    cs = cs + jnp.where(is_fox, carry_ref[...], 0.0)
    carry_ref[...] = cs[T - 1:T, :]
    o_ref[...] = jnp.where(lane < MLSTM_HEADS, pre, cs)


def _gates(z, bias, *, T):
    S = z.shape[0]
    return pl.pallas_call(
        _gates_kernel,
        grid=(S // T,),
        in_specs=[pl.BlockSpec((T, LANES), lambda c: (c, COL_GATES // LANES)),
                  pl.BlockSpec((1, N_GATES), lambda c: (0, 0))],
        out_specs=pl.BlockSpec((T, N_GATES), lambda c: (c, 0)),
        out_shape=jax.ShapeDtypeStruct((S, N_GATES), F32),
        scratch_shapes=[pltpu.VMEM((1, N_GATES), F32)],
        compiler_params=_cparams(("arbitrary",)),
        name="gates",
    )(z, bias)


POOL_HALO = 16


def _pool_kernel(u_ref, h_ref, w_ref, s_ref, o_ref):
    i = pl.program_id(0)
    T = u_ref.shape[0]
    u = u_ref[...]
    halo = jnp.where(i > 0, h_ref[...], 0.0)
    full = jnp.concatenate([halo, u], axis=0)
    t = i * T + lax.broadcasted_iota(jnp.int32, (T, 1), 0) + 1
    outs = []
    for g, win in enumerate(POOL_WINDOWS):
        cols = slice(g * POOL_CH, (g + 1) * POOL_CH)
        fg = full[:, cols]
        acc = fg[POOL_HALO:POOL_HALO + T]
        for j in range(1, win):
            acc = acc + fg[POOL_HALO - j:POOL_HALO - j + T]
        count = jnp.minimum(t, win).astype(F32)
        pooled = acc / count - u[:, cols]
        outs.append(_dot(pooled.astype(BF16), w_ref[g]))
    y = jnp.concatenate(outs, axis=1) * s_ref[...]
    o_ref[...] = y.astype(BF16)


def _pool(z, w, scale, *, T):
    S = z.shape[0]
    r = T // POOL_HALO
    return pl.pallas_call(
        _pool_kernel,
        grid=(S // T,),
        in_specs=[
            pl.BlockSpec((T, MIX_WIDTH), lambda i: (i, 0)),
            pl.BlockSpec((POOL_HALO, MIX_WIDTH), lambda i: (jnp.maximum(i * r - 1, 0), 0)),
            pl.BlockSpec((len(POOL_WINDOWS), POOL_CH, POOL_CH), lambda i: (0, 0, 0)),
            pl.BlockSpec((1, MIX_WIDTH), lambda i: (0, 0)),
        ],
        out_specs=pl.BlockSpec((T, MIX_WIDTH), lambda i: (i, 0)),
        out_shape=jax.ShapeDtypeStruct((S, MIX_WIDTH), BF16),
        compiler_params=_cparams(("parallel",)),
        name="pool",
    )(z, z, w, scale.reshape(1, MIX_WIDTH))


def _rope(x, cos, sin_signed):
    n = x.shape[1] // LANES
    lane = lax.broadcasted_iota(jnp.int32, (x.shape[0], LANES), 1)
    first_half = (lane % HEAD_DIM) < (HEAD_DIM // 2)
    outs = []
    for s in range(n):
        xs = x[:, s * LANES:(s + 1) * LANES]
        partner = jnp.where(first_half, pltpu.roll(xs, LANES - HEAD_DIM // 2, axis=1),
                            pltpu.roll(xs, HEAD_DIM // 2, axis=1))
        outs.append(xs * cos + partner * sin_signed)
    return outs[0] if n == 1 else jnp.concatenate(outs, axis=1)


def _swa_kernel(sink_ref, q_ref, k_ref, v_ref, kh_ref, vh_ref, cos_ref, sin_ref, cosh_ref, sinh_ref, o_ref, *, nblk):
    i = pl.program_id(0)
    B = SWA_BLOCK
    G = SWA_HEADS // SWA_KV_HEADS
    cos = cos_ref[...]
    sin = sin_ref[...]
    q = (_rope(q_ref[...], cos, sin) * (HEAD_DIM ** -0.5)).astype(BF16)
    k = jnp.concatenate([_rope(kh_ref[...], cosh_ref[...], sinh_ref[...]), _rope(k_ref[...], cos, sin)],
                        axis=0).astype(BF16)
    v = jnp.concatenate([vh_ref[...], v_ref[...]], axis=0).astype(BF16)

    lane = lax.broadcasted_iota(jnp.int32, (G * B, LANES), 1)
    low = lane < HEAD_DIM
    rowq = lax.broadcasted_iota(jnp.int32, (G * B, 2 * B), 0) % B
    colk = lax.broadcasted_iota(jnp.int32, (G * B, 2 * B), 1)
    band = (colk > rowq) & (colk <= rowq + B)
    grp = lax.broadcasted_iota(jnp.int32, (G * B, 1), 0) // B
    sinks = []
    for kvh in range(SWA_KV_HEADS):
        col = jnp.zeros((G * B, 1), F32)
        for g in range(G):
            col = jnp.where(grp == g, sink_ref[kvh * G + g], col)
        sinks.append(col)

    for blk in range(nblk):
        r0 = blk * B
        qb = jnp.concatenate([q[r0:r0 + B, s * LANES:(s + 1) * LANES] for s in range(G)], axis=0)
        kb = k[r0:r0 + 2 * B]
        vb = v[r0:r0 + 2 * B]
        valid = band
        if blk == 0:
            valid = band & ((colk >= B) | (i > 0))
        o_halves = []
        for kvh in range(SWA_KV_HEADS):
            qm = jnp.where(low if kvh == 0 else ~low, qb, jnp.zeros_like(qb))
            s = _dot_nt(qm, kb)
            s = jnp.where(valid, s, -jnp.inf)
            m = jnp.maximum(jnp.max(s, axis=1, keepdims=True), sinks[kvh])
            p = jnp.exp(s - m)
            den = jnp.sum(p, axis=1, keepdims=True) + jnp.exp(sinks[kvh] - m)
            o_halves.append(_dot(p.astype(BF16), vb) / den)
        o = jnp.where(low, o_halves[0], o_halves[1])
        for s_ in range(G):
            o_ref[r0:r0 + B, s_ * LANES:(s_ + 1) * LANES] = o[s_ * B:(s_ + 1) * B].astype(BF16)


def _swa(z, sinks, cos, sin, *, T):
    S = z.shape[0]
    nblk = T // SWA_BLOCK
    cq, ck, cv = COL_SWA_Q // MIX_WIDTH, COL_SWA_K // LANES, COL_SWA_V // LANES
    halo = lambda i: jnp.maximum(i * nblk - 1, 0)
    return pl.pallas_call(
        functools.partial(_swa_kernel, nblk=nblk),
        grid=(S // T,),
        in_specs=[
            pl.BlockSpec(memory_space=pltpu.SMEM),
            pl.BlockSpec((T, MIX_WIDTH), lambda i: (i, cq)),
            pl.BlockSpec((T, LANES), lambda i: (i, ck)),
            pl.BlockSpec((T, LANES), lambda i: (i, cv)),
            pl.BlockSpec((SWA_BLOCK, LANES), lambda i: (halo(i), ck)),
            pl.BlockSpec((SWA_BLOCK, LANES), lambda i: (halo(i), cv)),
            pl.BlockSpec((T, LANES), lambda i: (i, 0)),
            pl.BlockSpec((T, LANES), lambda i: (i, 0)),
            pl.BlockSpec((SWA_BLOCK, LANES), lambda i: (halo(i), 0)),
            pl.BlockSpec((SWA_BLOCK, LANES), lambda i: (halo(i), 0)),
        ],
        out_specs=pl.BlockSpec((T, MIX_WIDTH), lambda i: (i, 0)),
        out_shape=jax.ShapeDtypeStruct((S, MIX_WIDTH), BF16),
        compiler_params=_cparams(("parallel",)),
        name="swa",
    )(sinks, z, z, z, z, z, cos, sin, cos, sin)


CONV_HALO = 8


def _mlstm_kernel(q_ref, k_ref, v_ref, og_ref, qh_ref, kh_ref, cw_ref, gc_ref, gr_ref, o_ref, c_sc, n_sc, m_sc):
    c = pl.program_id(0)
    L = q_ref.shape[0]
    H, dh = MLSTM_HEADS, MLSTM_HEAD_DIM

    @pl.when(c == 0)
    def _():
        c_sc[...] = jnp.zeros_like(c_sc)
        n_sc[...] = jnp.zeros_like(n_sc)
        m_sc[...] = jnp.zeros_like(m_sc)

    def conv_silu(x_ref, halo_ref, w):
        halo = jnp.where(c > 0, halo_ref[...], 0.0)
        full = jnp.concatenate([halo, x_ref[...]], axis=0)
        acc = jnp.zeros((L, x_ref.shape[1]), F32)
        for j in range(MLSTM_CONV):
            off = CONV_HALO - (MLSTM_CONV - 1) + j
            acc = acc + w[j:j + 1, :] * full[off:off + L]
        return acc * jax.nn.sigmoid(acc)

    cw = cw_ref[...]
    qc = conv_silu(q_ref, qh_ref, cw[:, :H * dh]) * (dh ** -0.5)
    kc = conv_silu(k_ref, kh_ref, cw[:, H * dh:])
    vv = v_ref[...]
    gc = gc_ref[...]
    gr = gr_ref[...]
    row = lax.broadcasted_iota(jnp.int32, (L, L), 0)
    col = lax.broadcasted_iota(jnp.int32, (L, L), 1)
    causal = col <= row

    for h in range(H):
        sl = slice(h * dh, (h + 1) * dh)
        qf = qc[:, sl]
        qh, kh, vh = qf.astype(BF16), kc[:, sl].astype(BF16), vv[:, sl].astype(BF16)
        i_col = gc[:, h:h + 1]
        b_col = gc[:, H + h:H + h + 1]
        a_row = gr[h:h + 1, :] - gr[H + h:H + h + 1, :]
        m_prev = m_sc[h][0:1, 0:1]
        c_prev = c_sc[h]
        n_prev = n_sc[h]

        dlog = jnp.where(causal, b_col + a_row, -jnp.inf)
        inter = b_col + m_prev
        m_comb = jnp.maximum(inter, jnp.max(dlog, axis=1, keepdims=True))
        sm = _dot_nt(qh, kh) * jnp.exp(dlog - m_comb)
        w_inter = jnp.exp(inter - m_comb)
        num = _dot(sm.astype(BF16), vh) + w_inter * _dot_nt(qh, c_prev.astype(BF16))
        den = jnp.sum(sm, axis=1, keepdims=True) + w_inter * jnp.sum(qf * n_prev, axis=1, keepdims=True)
        hid = num / jnp.maximum(jnp.abs(den), jnp.exp(-m_comb))
        o_ref[:, sl] = (jax.nn.sigmoid(og_ref[:, sl]) * hid).astype(BF16)

        bl = b_col[L - 1:L, :]
        g = bl - b_col + i_col
        m_new = jnp.maximum(bl + m_prev, jnp.max(g, axis=0, keepdims=True))
        wk = jnp.exp(g - m_new)
        decay = jnp.exp(bl + m_prev - m_new)
        c_sc[h] = decay * c_prev + _dot_tn((wk * vv[:, sl]).astype(BF16), kh)
        n_sc[h] = decay * n_prev + jnp.sum(wk * kc[:, sl], axis=0, keepdims=True)
        m_sc[h] = jnp.broadcast_to(m_new, m_sc.shape[1:])


def _mlstm(z, conv_w, gates_col, gates_row, *, L):
    S = z.shape[0]
    r = L // CONV_HALO
    W = MLSTM_HEADS * MLSTM_HEAD_DIM
    halo = lambda c: jnp.maximum(c * r - 1, 0)
    return pl.pallas_call(
        _mlstm_kernel,
        grid=(S // L,),
        in_specs=[
            pl.BlockSpec((L, W), lambda c: (c, COL_ML_Q // W)),
            pl.BlockSpec((L, W), lambda c: (c, COL_ML_K // W)),
            pl.BlockSpec((L, W), lambda c: (c, COL_ML_V // W)),
            pl.BlockSpec((L, W), lambda c: (c, COL_ML_O // W)),
            pl.BlockSpec((CONV_HALO, W), lambda c: (halo(c), COL_ML_Q // W)),
            pl.BlockSpec((CONV_HALO, W), lambda c: (halo(c), COL_ML_K // W)),
            pl.BlockSpec((MLSTM_CONV, 2 * W), lambda c: (0, 0)),
            pl.BlockSpec((L, N_GATES), lambda c: (c, 0)),
            pl.BlockSpec((N_GATES, L), lambda c: (0, c)),
        ],
        out_specs=pl.BlockSpec((L, W), lambda c: (c, 0)),
        out_shape=jax.ShapeDtypeStruct((S, W), BF16),
        scratch_shapes=[pltpu.VMEM((MLSTM_HEADS, MLSTM_HEAD_DIM, MLSTM_HEAD_DIM), F32),
                        pltpu.VMEM((MLSTM_HEADS, 1, MLSTM_HEAD_DIM), F32),
                        pltpu.VMEM((MLSTM_HEADS, 8, LANES), F32)],
        compiler_params=_cparams(("arbitrary",)),
        name="mlstm",
    )(z, z, z, z, z, z, conv_w, gates_col, gates_row)


def _fox_kernel(qi_ref, kj_ref, q_ref, k_ref, v_ref, cq_ref, ck_ref, o_ref, m_sc, l_sc, acc_sc, *, tq, tk):
    t = pl.program_id(1)
    i = qi_ref[t]
    j = kj_ref[t]

    @pl.when(j == 0)
    def _():
        m_sc[...] = jnp.full_like(m_sc, -jnp.inf)
        l_sc[...] = jnp.zeros_like(l_sc)
        acc_sc[...] = jnp.zeros_like(acc_sc)

    q = q_ref[...] * (HEAD_DIM ** -0.5)
    kb = k_ref[...].astype(BF16)
    vb = v_ref[...].astype(BF16)
    cq = cq_ref[...]
    ck = ck_ref[...]
    lane = lax.broadcasted_iota(jnp.int32, (tq, LANES), 1)
    low = lane < HEAD_DIM
    row = lax.broadcasted_iota(jnp.int32, (tq, tk), 0)
    col = lax.broadcasted_iota(jnp.int32, (tq, tk), 1)
    causal = (col - row) <= (i * tq - j * tk)

    pv, alphas = [], []
    for a in range(2):
        qa = jnp.where(low if a == 0 else ~low, q, 0.0).astype(BF16)
        s = _dot_nt(qa, kb) + cq[:, a:a + 1] - ck[a:a + 1, :]
        s = jnp.where(causal, s, -jnp.inf)
        m_old = m_sc[a]
        m_new = jnp.maximum(m_old, jnp.max(s, axis=1, keepdims=True))
        alpha = jnp.exp(m_old - m_new)
        p = jnp.exp(s - m_new)
        l_sc[a] = alpha * l_sc[a] + jnp.sum(p, axis=1, keepdims=True)
        m_sc[a] = m_new
        pv.append(_dot(p.astype(BF16), vb))
        alphas.append(alpha)
    acc_sc[...] = acc_sc[...] * jnp.where(low, alphas[0], alphas[1]) + jnp.where(low, pv[0], pv[1])

    @pl.when((j + 1) * tk >= (i + 1) * tq)
    def _():
        o_ref[...] = (acc_sc[...] / jnp.where(low, l_sc[0], l_sc[1])).astype(BF16)


def _fox(z, c_col, c_row, *, tq, tk):
    S = z.shape[0]
    nq = S // tq
    qi, kj = [], []
    for i in range(nq):
        for j in range(((i + 1) * tq + tk - 1) // tk):
            qi.append(i)
            kj.append(j)
    qi = jnp.asarray(np.array(qi, np.int32))
    kj = jnp.asarray(np.array(kj, np.int32))
    npair = FOX_HEADS // 2
    cq0, ck0, cv0 = COL_FOX_Q // LANES, COL_FOX_K // LANES, COL_FOX_V // LANES
    grid_spec = pltpu.PrefetchScalarGridSpec(
        num_scalar_prefetch=2,
        grid=(npair, int(qi.shape[0])),
        in_specs=[
            pl.BlockSpec((tq, LANES), lambda p, t, qi, kj: (qi[t], cq0 + p)),
            pl.BlockSpec((tk, LANES), lambda p, t, qi, kj: (kj[t], ck0 + p)),
            pl.BlockSpec((tk, LANES), lambda p, t, qi, kj: (kj[t], cv0 + p)),
            pl.BlockSpec((None, tq, 2), lambda p, t, qi, kj: (p, qi[t], 0)),
            pl.BlockSpec((None, 2, tk), lambda p, t, qi, kj: (p, 0, kj[t])),
        ],
        out_specs=pl.BlockSpec((tq, LANES), lambda p, t, qi, kj: (qi[t], p)),
        scratch_shapes=[pltpu.VMEM((2, tq, 1), F32), pltpu.VMEM((2, tq, 1), F32), pltpu.VMEM((tq, LANES), F32)],
    )
    return pl.pallas_call(
        functools.partial(_fox_kernel, tq=tq, tk=tk),
        grid_spec=grid_spec,
        out_shape=jax.ShapeDtypeStruct((S, MIX_WIDTH), BF16),
        compiler_params=_cparams(("parallel", "arbitrary")),
        name="fox",
    )(qi, kj, z, z, z, c_col, c_row)


def _merge_kernel(hb_ref, ya_ref, yb_ref, yc_ref, yd_ref, wg_ref, wb_ref, o_ref):
    hb = hb_ref[...]
    acc = None
    for b, y_ref in enumerate((ya_ref, yb_ref, yc_ref, yd_ref)):
        gate = jax.nn.sigmoid(_dot(hb, wg_ref[b]))
        term = gate * _dot(y_ref[...], wb_ref[b])
        acc = term if acc is None else acc + term
    o_ref[...] = acc.astype(BF16)


def _merge(hb, ys, wg, wb, *, tm, tn):
    S, D = hb.shape
    nb, W, _ = wb.shape
    yspec = pl.BlockSpec((tm, W), lambda n, i: (i, 0))
    return pl.pallas_call(
        _merge_kernel,
        grid=(D // tn, S // tm),
        in_specs=[pl.BlockSpec((tm, D), lambda n, i: (i, 0)), yspec, yspec, yspec, yspec,
                  pl.BlockSpec((nb, D, tn), lambda n, i: (0, 0, n)),
                  pl.BlockSpec((nb, W, tn), lambda n, i: (0, 0, n))],
        out_specs=pl.BlockSpec((tm, tn), lambda n, i: (i, n)),
        out_shape=jax.ShapeDtypeStruct((S, D), BF16),
        compiler_params=_cparams(("parallel", "arbitrary")),
        name="merge",
    )(hb, *ys, wg, wb)


def _outproj_kernel(m_ref, h_ref, w_ref, g_ref, b_ref, o_ref, *, alpha):
    y = alpha * h_ref[...] + _dot(m_ref[...], w_ref[...])
    o_ref[...] = _layer_norm(y, g_ref[...], b_ref[...])


def _outproj(merged, h, w, g, b, *, alpha, tm):
    S, D = h.shape
    return pl.pallas_call(
        functools.partial(_outproj_kernel, alpha=alpha),
        grid=(S // tm,),
        in_specs=[pl.BlockSpec((tm, D), lambda i: (i, 0)), pl.BlockSpec((tm, D), lambda i: (i, 0)),
                  pl.BlockSpec((D, D), lambda i: (0, 0)),
                  pl.BlockSpec((1, D), lambda i: (0, 0)), pl.BlockSpec((1, D), lambda i: (0, 0))],
        out_specs=pl.BlockSpec((tm, D), lambda i: (i, 0)),
        out_shape=jax.ShapeDtypeStruct((S, D), F32),
        compiler_params=_cparams(("parallel",)),
        name="outproj",
    )(merged, h, w, g.reshape(1, D), b.reshape(1, D))


def _swa_head_order():
    g = SWA_HEADS // SWA_KV_HEADS
    order = []
    for s in range(g):
        order += [s, g + s]
    return order


def _swa_col_perm():
    return np.concatenate([np.arange(h * HEAD_DIM, (h + 1) * HEAD_DIM) for h in _swa_head_order()])


def _rope_tables(S):
    half = HEAD_DIM // 2
    inv = ROPE_THETA ** (-jnp.arange(half, dtype=F32) / half)
    ang = jnp.arange(S, dtype=jnp.int32).astype(F32)[:, None] * inv[None, :]
    cos, sin = jnp.cos(ang), jnp.sin(ang)
    reps = LANES // HEAD_DIM
    cos_t = jnp.tile(jnp.concatenate([cos, cos], axis=1), (1, reps))
    sin_t = jnp.tile(jnp.concatenate([-sin, sin], axis=1), (1, reps))
    return cos_t, sin_t


def _pick(n, prefs):
    for p in prefs:
        if n % p == 0:
            return p
    return n


def kernel(x, ln_g, ln_b, ffn1_w_in, ffn1_w_out, w_in, pool_w, pool_scale, swa_sinks, mlstm_conv, mlstm_i_bias, mlstm_f_bias, fox_f_bias, w_branch, w_gate, w_out, ffn2_w_in, ffn2_w_out):
    B, S, D = x.shape
    depth = ln_g.shape[0]
    assert B == 1 and w_in.shape[2] == 4880 and S % ML_CHUNK == 0
    alpha = float((2 * depth) ** 0.25)
    perm = _swa_col_perm()
    cos_t, sin_t = _rope_tables(S)
    tm = _pick(S, (512, 256))
    t_mix = _pick(S, (512, 256))
    tq = _pick(S, (512, 256))

    h = x.reshape(S, D)
    for l in range(depth):
        wz = w_in[l]
        wz = jnp.concatenate([
            wz[:, 0:512], wz[:, 512:1024][:, perm], wz[:, 1280:3328], wz[:, 3336:4872], wz[:, 1024:1280],
            wz[:, 3328:3336], wz[:, 4872:4880], jnp.zeros((D, Z_COLS - COL_GATES - N_GATES), wz.dtype)],
            axis=1).astype(BF16)
        wb = w_branch[l].at[1].set(w_branch[l, 1][perm, :]).astype(BF16)
        wg = w_gate[l].astype(BF16)
        wo = w_out[l].astype(BF16)
        gate_bias = jnp.concatenate([mlstm_i_bias[l], mlstm_f_bias[l], fox_f_bias[l]]).reshape(1, N_GATES)
        sinks = swa_sinks[l]

        h, hb = _ffn(h, ffn1_w_in[l].astype(BF16), ffn1_w_out[l].astype(BF16), ln_g[l, 0], ln_b[l, 0],
                     alpha=alpha, tm=tm, tf=512)
        z = _inproj(hb, wz, tm=tm, tn=Z_COLS // 3)
        gates_col = _gates(z, gate_bias, T=ML_CHUNK)
        gates_row = gates_col.T
        nh = FOX_HEADS // 2
        c_col = gates_col[:, 2 * MLSTM_HEADS:].reshape(S, nh, 2).transpose(1, 0, 2)
        c_row = gates_row[2 * MLSTM_HEADS:].reshape(nh, 2, S)

        ya = _pool(z, pool_w[l].astype(BF16), pool_scale[l], T=t_mix)
        yb = _swa(z, sinks, cos_t, sin_t, T=t_mix)
        yc = _mlstm(z, mlstm_conv[l], gates_col, gates_row, L=ML_CHUNK)
        yd = _fox(z, c_col, c_row, tq=tq, tk=tq)
        merged = _merge(hb, (ya, yb, yc, yd), wg, wb, tm=tm, tn=256)
        h = _outproj(merged, h, wo, ln_g[l, 1], ln_b[l, 1], alpha=alpha, tm=tm)
        h, hb = _ffn(h, ffn2_w_in[l].astype(BF16), ffn2_w_out[l].astype(BF16), ln_g[l, 2], ln_b[l, 2],
                     alpha=alpha, tm=tm, tf=512)
    return h.reshape(B, S, D)
```

```python
import functools

import jax
import jax.numpy as jnp
import numpy as np
from jax import lax
from jax.experimental import pallas as pl
from jax.experimental.pallas import tpu as pltpu

F32 = jnp.float32
BF16 = jnp.bfloat16

MIX_WIDTH = 512
HEAD_DIM = 64
POOL_WINDOWS = (2, 4, 8, 16)
POOL_CH = 128
SWA_HEADS = 8
SWA_KV_HEADS = 2
SWA_BLOCK = 128
ROPE_THETA = 10000.0
MLSTM_HEADS = 4
MLSTM_HEAD_DIM = 128
MLSTM_CONV = 4
FOX_HEADS = 8
LN_EPS = 1e-5
FFN_HALF = 0.5

LANES = 128
VMEM_LIMIT = 56 * 1024 * 1024

COL_POOL = 0
COL_SWA_Q = 512
COL_ML_Q = 1024
COL_ML_K = 1536
COL_ML_V = 2048
COL_ML_O = 2560
COL_FOX_Q = 3072
COL_FOX_K = 3584
COL_FOX_V = 4096
COL_SWA_K = 4608
COL_SWA_V = 4736
COL_GATES = 4864
Z_COLS = 4992
N_GATES = 16

ML_CHUNK = 256


def _cparams(sem):
    return pltpu.CompilerParams(dimension_semantics=sem, vmem_limit_bytes=VMEM_LIMIT)


def _layer_norm(y, g, b):
    mu = jnp.mean(y, axis=-1, keepdims=True)
    d = y - mu
    var = jnp.mean(d * d, axis=-1, keepdims=True)
    return d * lax.rsqrt(var + LN_EPS) * g + b


def _dot(a, b):
    return jnp.dot(a, b, preferred_element_type=F32)


def _dot_nt(a, b):
    return lax.dot_general(a, b, (((1,), (1,)), ((), ())), preferred_element_type=F32)


def _dot_tn(a, b):
    return lax.dot_general(a, b, (((0,), (0,)), ((), ())), preferred_element_type=F32)


def _ffn_kernel(x_ref, wg_ref, wu_ref, wo_ref, g_ref, b_ref, o_ref, ob_ref, xb_ref, acc_ref, *, alpha, nf):
    f = pl.program_id(1)

    @pl.when(f == 0)
    def _():
        xb_ref[...] = x_ref[...].astype(BF16)
        acc_ref[...] = jnp.zeros_like(acc_ref)

    xb = xb_ref[...]
    gate = _dot(xb, wg_ref[...])
    up = _dot(xb, wu_ref[...])
    act = (gate * jax.nn.sigmoid(gate) * up).astype(BF16)
    acc_ref[...] += _dot(act, wo_ref[...])

    @pl.when(f == nf - 1)
    def _():
        y = alpha * x_ref[...] + FFN_HALF * acc_ref[...]
        o = _layer_norm(y, g_ref[...], b_ref[...])
        o_ref[...] = o
        ob_ref[...] = o.astype(BF16)


def _ffn(x, w_in, w_out, g, b, *, alpha, tm, tf):
    S, D = x.shape
    DF = w_out.shape[0]
    nf = DF // tf
    assert S % tm == 0 and DF % tf == 0
    return pl.pallas_call(
        functools.partial(_ffn_kernel, alpha=alpha, nf=nf),
        grid=(S // tm, nf),
        in_specs=[
            pl.BlockSpec((tm, D), lambda i, f: (i, 0)),
            pl.BlockSpec((D, tf), lambda i, f: (0, f)),
            pl.BlockSpec((D, tf), lambda i, f: (0, f + nf)),
            pl.BlockSpec((tf, D), lambda i, f: (f, 0)),
            pl.BlockSpec((1, D), lambda i, f: (0, 0)),
            pl.BlockSpec((1, D), lambda i, f: (0, 0)),
        ],
        out_specs=[pl.BlockSpec((tm, D), lambda i, f: (i, 0)), pl.BlockSpec((tm, D), lambda i, f: (i, 0))],
        out_shape=[jax.ShapeDtypeStruct((S, D), F32), jax.ShapeDtypeStruct((S, D), BF16)],
        scratch_shapes=[pltpu.VMEM((tm, D), BF16), pltpu.VMEM((tm, D), F32)],
        compiler_params=_cparams(("parallel", "arbitrary")),
        name="ffn",
    )(x, w_in, w_in, w_out, g.reshape(1, D), b.reshape(1, D))


def _inproj_kernel(x_ref, w_ref, o_ref, ob_ref):
    z = _dot(x_ref[...], w_ref[...])
    o_ref[...] = z
    ob_ref[...] = z.astype(BF16)


def _inproj(xb, w, *, tm, tn):
    S, D = xb.shape
    N = w.shape[1]
    assert S % tm == 0 and N % tn == 0
    ospec = pl.BlockSpec((tm, tn), lambda i, j: (i, j))
    return pl.pallas_call(
        _inproj_kernel,
        grid=(S // tm, N // tn),
        in_specs=[pl.BlockSpec((tm, D), lambda i, j: (i, 0)), pl.BlockSpec((D, tn), lambda i, j: (0, j))],
        out_specs=[ospec, ospec],
        out_shape=[jax.ShapeDtypeStruct((S, N), F32), jax.ShapeDtypeStruct((S, N), BF16)],
        compiler_params=_cparams(("parallel", "arbitrary")),
        name="inproj",
    )(xb, w)


def _split3(x):
    hi = x.astype(BF16)
    r1 = x - hi.astype(F32)
    mid = r1.astype(BF16)
    lo = (r1 - mid.astype(F32)).astype(BF16)
    return hi, mid, lo


def _gates_kernel(z_ref, bias_ref, o_ref, carry_ref):
    c = pl.program_id(0)
    T = z_ref.shape[0]

    @pl.when(c == 0)
    def _():
        carry_ref[...] = jnp.zeros_like(carry_ref)

    pre = z_ref[:, 0:N_GATES] + bias_ref[...]
    lf = jax.nn.log_sigmoid(pre)
    row = lax.broadcasted_iota(jnp.int32, (T, T), 0)
    col = lax.broadcasted_iota(jnp.int32, (T, T), 1)
    tri = jnp.where(col <= row, 1.0, 0.0).astype(BF16)
    hi, mid, lo = _split3(lf)
    cs = _dot(tri, hi) + _dot(tri, mid) + _dot(tri, lo)
    lane = lax.broadcasted_iota(jnp.int32, (T, N_GATES), 1)
    is_fox = lane >= 2 * MLSTM_HEADS
    cs = cs + jnp.where(is_fox, carry_ref[...], 0.0)
    carry_ref[...] = cs[T - 1:T, :]
    o_ref[...] = jnp.where(lane < MLSTM_HEADS, pre, cs)


def _gates(z, bias, *, T):
    S = z.shape[0]
    return pl.pallas_call(
        _gates_kernel,
        grid=(S // T,),
        in_specs=[pl.BlockSpec((T, LANES), lambda c: (c, COL_GATES // LANES)),
                  pl.BlockSpec((1, N_GATES), lambda c: (0, 0))],
        out_specs=pl.BlockSpec((T, N_GATES), lambda c: (c, 0)),
        out_shape=jax.ShapeDtypeStruct((S, N_GATES), F32),
        scratch_shapes=[pltpu.VMEM((1, N_GATES), F32)],
        compiler_params=_cparams(("arbitrary",)),
        name="gates",
    )(z, bias)


POOL_HALO = 16


def _pool_kernel(u_ref, h_ref, w_ref, s_ref, o_ref):
    i = pl.program_id(0)
    T = u_ref.shape[0]
    u = u_ref[...]
    halo = jnp.where(i > 0, h_ref[...], 0.0)
    full = jnp.concatenate([halo, u], axis=0)
    t = i * T + lax.broadcasted_iota(jnp.int32, (T, 1), 0) + 1
    outs = []
    for g, win in enumerate(POOL_WINDOWS):
        cols = slice(g * POOL_CH, (g + 1) * POOL_CH)
        fg = full[:, cols]
        acc = fg[POOL_HALO:POOL_HALO + T]
        for j in range(1, win):
            acc = acc + fg[POOL_HALO - j:POOL_HALO - j + T]
        count = jnp.minimum(t, win).astype(F32)
        pooled = acc / count - u[:, cols]
        outs.append(_dot(pooled.astype(BF16), w_ref[g]))
    y = jnp.concatenate(outs, axis=1) * s_ref[...]
    o_ref[...] = y.astype(BF16)


def _pool(z, w, scale, *, T):
    S = z.shape[0]
    r = T // POOL_HALO
    return pl.pallas_call(
        _pool_kernel,
        grid=(S // T,),
        in_specs=[
            pl.BlockSpec((T, MIX_WIDTH), lambda i: (i, 0)),
            pl.BlockSpec((POOL_HALO, MIX_WIDTH), lambda i: (jnp.maximum(i * r - 1, 0), 0)),
            pl.BlockSpec((len(POOL_WINDOWS), POOL_CH, POOL_CH), lambda i: (0, 0, 0)),
            pl.BlockSpec((1, MIX_WIDTH), lambda i: (0, 0)),
        ],
        out_specs=pl.BlockSpec((T, MIX_WIDTH), lambda i: (i, 0)),
        out_shape=jax.ShapeDtypeStruct((S, MIX_WIDTH), BF16),
        compiler_params=_cparams(("parallel",)),
        name="pool",
    )(z, z, w, scale.reshape(1, MIX_WIDTH))


def _rope(x, cos, sin_signed):
    n = x.shape[1] // LANES
    lane = lax.broadcasted_iota(jnp.int32, (x.shape[0], LANES), 1)
    first_half = (lane % HEAD_DIM) < (HEAD_DIM // 2)
    outs = []
    for s in range(n):
        xs = x[:, s * LANES:(s + 1) * LANES]
        partner = jnp.where(first_half, pltpu.roll(xs, LANES - HEAD_DIM // 2, axis=1),
                            pltpu.roll(xs, HEAD_DIM // 2, axis=1))
        outs.append(xs * cos + partner * sin_signed)
    return outs[0] if n == 1 else jnp.concatenate(outs, axis=1)


def _swa_kernel(sink_ref, q_ref, k_ref, v_ref, kh_ref, vh_ref, cos_ref, sin_ref, cosh_ref, sinh_ref, o_ref, *, nblk):
    i = pl.program_id(0)
    B = SWA_BLOCK
    G = SWA_HEADS // SWA_KV_HEADS
    cos = cos_ref[...]
    sin = sin_ref[...]
    q = (_rope(q_ref[...], cos, sin) * (HEAD_DIM ** -0.5)).astype(BF16)
    k = jnp.concatenate([_rope(kh_ref[...], cosh_ref[...], sinh_ref[...]), _rope(k_ref[...], cos, sin)],
                        axis=0).astype(BF16)
    v = jnp.concatenate([vh_ref[...], v_ref[...]], axis=0).astype(BF16)

    lane = lax.broadcasted_iota(jnp.int32, (G * B, LANES), 1)
    low = lane < HEAD_DIM
    rowq = lax.broadcasted_iota(jnp.int32, (G * B, 2 * B), 0) % B
    colk = lax.broadcasted_iota(jnp.int32, (G * B, 2 * B), 1)
    band = (colk > rowq) & (colk <= rowq + B)
    grp = lax.broadcasted_iota(jnp.int32, (G * B, 1), 0) // B
    sinks = []
    for kvh in range(SWA_KV_HEADS):
        col = jnp.zeros((G * B, 1), F32)
        for g in range(G):
            col = jnp.where(grp == g, sink_ref[kvh * G + g], col)
        sinks.append(col)

    for blk in range(nblk):
        r0 = blk * B
        qb = jnp.concatenate([q[r0:r0 + B, s * LANES:(s + 1) * LANES] for s in range(G)], axis=0)
        kb = k[r0:r0 + 2 * B]
        vb = v[r0:r0 + 2 * B]
        valid = band
        if blk == 0:
            valid = band & ((colk >= B) | (i > 0))
        o_halves = []
        for kvh in range(SWA_KV_HEADS):
            qm = jnp.where(low if kvh == 0 else ~low, qb, jnp.zeros_like(qb))
            s = _dot_nt(qm, kb)
            s = jnp.where(valid, s, -jnp.inf)
            m = jnp.maximum(jnp.max(s, axis=1, keepdims=True), sinks[kvh])
            p = jnp.exp(s - m)
            den = jnp.sum(p, axis=1, keepdims=True) + jnp.exp(sinks[kvh] - m)
            o_halves.append(_dot(p.astype(BF16), vb) / den)
        o = jnp.where(low, o_halves[0], o_halves[1])
        for s_ in range(G):
            o_ref[r0:r0 + B, s_ * LANES:(s_ + 1) * LANES] = o[s_ * B:(s_ + 1) * B].astype(BF16)


def _swa(z, sinks, cos, sin, *, T):
    S = z.shape[0]
    nblk = T // SWA_BLOCK
    cq, ck, cv = COL_SWA_Q // MIX_WIDTH, COL_SWA_K // LANES, COL_SWA_V // LANES
    halo = lambda i: jnp.maximum(i * nblk - 1, 0)
    return pl.pallas_call(
        functools.partial(_swa_kernel, nblk=nblk),
        grid=(S // T,),
        in_specs=[
            pl.BlockSpec(memory_space=pltpu.SMEM),
            pl.BlockSpec((T, MIX_WIDTH), lambda i: (i, cq)),
            pl.BlockSpec((T, LANES), lambda i: (i, ck)),
            pl.BlockSpec((T, LANES), lambda i: (i, cv)),
            pl.BlockSpec((SWA_BLOCK, LANES), lambda i: (halo(i), ck)),
            pl.BlockSpec((SWA_BLOCK, LANES), lambda i: (halo(i), cv)),
            pl.BlockSpec((T, LANES), lambda i: (i, 0)),
            pl.BlockSpec((T, LANES), lambda i: (i, 0)),
            pl.BlockSpec((SWA_BLOCK, LANES), lambda i: (halo(i), 0)),
            pl.BlockSpec((SWA_BLOCK, LANES), lambda i: (halo(i), 0)),
        ],
        out_specs=pl.BlockSpec((T, MIX_WIDTH), lambda i: (i, 0)),
        out_shape=jax.ShapeDtypeStruct((S, MIX_WIDTH), BF16),
        compiler_params=_cparams(("parallel",)),
        name="swa",
    )(sinks, z, z, z, z, z, cos, sin, cos, sin)


CONV_HALO = 8


def _mlstm_kernel(q_ref, k_ref, v_ref, og_ref, qh_ref, kh_ref, cw_ref, gc_ref, gr_ref, o_ref, c_sc, n_sc, m_sc):
    c = pl.program_id(0)
    L = q_ref.shape[0]
    H, dh = MLSTM_HEADS, MLSTM_HEAD_DIM

    @pl.when(c == 0)
    def _():
        c_sc[...] = jnp.zeros_like(c_sc)
        n_sc[...] = jnp.zeros_like(n_sc)
        m_sc[...] = jnp.zeros_like(m_sc)

    def conv_silu(x_ref, halo_ref, w):
        halo = jnp.where(c > 0, halo_ref[...], 0.0)
        full = jnp.concatenate([halo, x_ref[...]], axis=0)
        acc = jnp.zeros((L, x_ref.shape[1]), F32)
        for j in range(MLSTM_CONV):
            off = CONV_HALO - (MLSTM_CONV - 1) + j
            acc = acc + w[j:j + 1, :] * full[off:off + L]
        return acc * jax.nn.sigmoid(acc)

    cw = cw_ref[...]
    qc = conv_silu(q_ref, qh_ref, cw[:, :H * dh]) * (dh ** -0.5)
    kc = conv_silu(k_ref, kh_ref, cw[:, H * dh:])
    vv = v_ref[...]
    gc = gc_ref[...]
    gr = gr_ref[...]
    row = lax.broadcasted_iota(jnp.int32, (L, L), 0)
    col = lax.broadcasted_iota(jnp.int32, (L, L), 1)
    causal = col <= row

    for h in range(H):
        sl = slice(h * dh, (h + 1) * dh)
        qf = qc[:, sl]
        qh, kh, vh = qf.astype(BF16), kc[:, sl].astype(BF16), vv[:, sl].astype(BF16)
        i_col = gc[:, h:h + 1]
        b_col = gc[:, H + h:H + h + 1]
        a_row = gr[h:h + 1, :] - gr[H + h:H + h + 1, :]
        m_prev = m_sc[h][0:1, 0:1]
        c_prev = c_sc[h]
        n_prev = n_sc[h]

        dlog = jnp.where(causal, b_col + a_row, -jnp.inf)
        inter = b_col + m_prev
        m_comb = jnp.maximum(inter, jnp.max(dlog, axis=1, keepdims=True))
        sm = _dot_nt(qh, kh) * jnp.exp(dlog - m_comb)
        w_inter = jnp.exp(inter - m_comb)
        num = _dot(sm.astype(BF16), vh) + w_inter * _dot_nt(qh, c_prev.astype(BF16))
        den = jnp.sum(sm, axis=1, keepdims=True) + w_inter * jnp.sum(qf * n_prev, axis=1, keepdims=True)
        hid = num / jnp.maximum(jnp.abs(den), jnp.exp(-m_comb))
        o_ref[:, sl] = (jax.nn.sigmoid(og_ref[:, sl]) * hid).astype(BF16)

        bl = b_col[L - 1:L, :]
        g = bl - b_col + i_col
        m_new = jnp.maximum(bl + m_prev, jnp.max(g, axis=0, keepdims=True))
        wk = jnp.exp(g - m_new)
        decay = jnp.exp(bl + m_prev - m_new)
        c_sc[h] = decay * c_prev + _dot_tn((wk * vv[:, sl]).astype(BF16), kh)
        n_sc[h] = decay * n_prev + jnp.sum(wk * kc[:, sl], axis=0, keepdims=True)
        m_sc[h] = jnp.broadcast_to(m_new, m_sc.shape[1:])


def _mlstm(z, conv_w, gates_col, gates_row, *, L):
    S = z.shape[0]
    r = L // CONV_HALO
    W = MLSTM_HEADS * MLSTM_HEAD_DIM
    halo = lambda c: jnp.maximum(c * r - 1, 0)
    return pl.pallas_call(
        _mlstm_kernel,
        grid=(S // L,),
        in_specs=[
            pl.BlockSpec((L, W), lambda c: (c, COL_ML_Q // W)),
            pl.BlockSpec((L, W), lambda c: (c, COL_ML_K // W)),
            pl.BlockSpec((L, W), lambda c: (c, COL_ML_V // W)),
            pl.BlockSpec((L, W), lambda c: (c, COL_ML_O // W)),
            pl.BlockSpec((CONV_HALO, W), lambda c: (halo(c), COL_ML_Q // W)),
            pl.BlockSpec((CONV_HALO, W), lambda c: (halo(c), COL_ML_K // W)),
            pl.BlockSpec((MLSTM_CONV, 2 * W), lambda c: (0, 0)),
            pl.BlockSpec((L, N_GATES), lambda c: (c, 0)),
            pl.BlockSpec((N_GATES, L), lambda c: (0, c)),
        ],
        out_specs=pl.BlockSpec((L, W), lambda c: (c, 0)),
        out_shape=jax.ShapeDtypeStruct((S, W), BF16),
        scratch_shapes=[pltpu.VMEM((MLSTM_HEADS, MLSTM_HEAD_DIM, MLSTM_HEAD_DIM), F32),
                        pltpu.VMEM((MLSTM_HEADS, 1, MLSTM_HEAD_DIM), F32),
                        pltpu.VMEM((MLSTM_HEADS, 8, LANES), F32)],
        compiler_params=_cparams(("arbitrary",)),
        name="mlstm",
    )(z, z, z, z, z, z, conv_w, gates_col, gates_row)


LOG2E = 1.4426950408889634


def _fox_kernel(q_ref, k_ref, v_ref, cq_ref, ck_ref, o_ref, m_sc, acc_sc, *, t):
    i = pl.program_id(1)
    nch = t // LANES
    m_sc[...] = jnp.full_like(m_sc, -jnp.inf)
    acc_sc[...] = jnp.zeros_like(acc_sc)

    q = q_ref[...] * (LOG2E * HEAD_DIM ** -0.5)
    lane = lax.broadcasted_iota(jnp.int32, (t, LANES), 1)
    row = lax.broadcasted_iota(jnp.int32, (t, LANES), 0)
    low = lane < HEAD_DIM
    own = [low, ~low]
    qa = [jnp.where(own[a], q, 0.0).astype(BF16) for a in range(2)]
    cq = cq_ref[...] * LOG2E
    ca = [jnp.broadcast_to(cq[:, a:a + 1], (t, LANES)) for a in range(2)]

    def tile(j, masked):
        j0 = pl.multiple_of(j * t, t)
        kb = k_ref[pl.ds(j0, t), :]
        vb = v_ref[pl.ds(j0, t), :]
        ck = ck_ref[:, pl.ds(j0, t)] * LOG2E
        for a in range(2):
            s = _dot_nt(qa[a], kb)
            sc = []
            for c in range(nch):
                x = s[:, c * LANES:(c + 1) * LANES] - ck[a:a + 1, c * LANES:(c + 1) * LANES]
                if masked:
                    x = jnp.where(lane + c * LANES <= row, x, -jnp.inf)
                sc.append(x)
            mx = sc[0]
            for c in range(1, nch):
                mx = jnp.maximum(mx, sc[c])
            m_old = m_sc[a]
            m_new = jnp.maximum(m_old, jnp.max(mx, axis=1, keepdims=True) + ca[a])
            alpha = jnp.exp2(m_old - m_new)
            mm = m_new - ca[a]
            p = jnp.concatenate([jnp.exp2(x - mm).astype(BF16) for x in sc], axis=1)
            va = jnp.where(own[a], vb, jnp.ones_like(vb))
            acc_sc[a] = alpha * acc_sc[a] + _dot(p, va)
            m_sc[a] = m_new

    def body(j, carry):
        tile(j, False)
        return carry

    lax.fori_loop(0, i, body, 0)
    tile(i, True)
    outs = []
    for a in range(2):
        acc = acc_sc[a]
        outs.append(acc / pltpu.roll(acc, HEAD_DIM, axis=1))
    o_ref[...] = jnp.where(low, outs[0], outs[1]).astype(BF16)


def _fox(z, zb, c_col, c_row, *, t):
    S = z.shape[0]
    npair = FOX_HEADS // 2
    cq0, ck0, cv0 = COL_FOX_Q // LANES, COL_FOX_K // LANES, COL_FOX_V // LANES
    return pl.pallas_call(
        functools.partial(_fox_kernel, t=t),
        grid=(npair, S // t),
        in_specs=[
            pl.BlockSpec((t, LANES), lambda p, i: (i, cq0 + p)),
            pl.BlockSpec((S, LANES), lambda p, i: (0, ck0 + p)),
            pl.BlockSpec((S, LANES), lambda p, i: (0, cv0 + p)),
            pl.BlockSpec((None, t, 2), lambda p, i: (p, i, 0)),
            pl.BlockSpec((None, 2, S), lambda p, i: (p, 0, 0)),
        ],
        out_specs=pl.BlockSpec((t, LANES), lambda p, i: (i, p)),
        out_shape=jax.ShapeDtypeStruct((S, MIX_WIDTH), BF16),
        scratch_shapes=[pltpu.VMEM((2, t, LANES), F32), pltpu.VMEM((2, t, LANES), F32)],
        compiler_params=_cparams(("parallel", "arbitrary")),
        name="fox",
    )(z, zb, zb, c_col, c_row)


def _merge_kernel(hb_ref, ya_ref, yb_ref, yc_ref, yd_ref, wg_ref, wb_ref, o_ref):
    hb = hb_ref[...]
    acc = None
    for b, y_ref in enumerate((ya_ref, yb_ref, yc_ref, yd_ref)):
        gate = jax.nn.sigmoid(_dot(hb, wg_ref[b]))
        term = gate * _dot(y_ref[...], wb_ref[b])
        acc = term if acc is None else acc + term
    o_ref[...] = acc.astype(BF16)


def _merge(hb, ys, wg, wb, *, tm, tn):
    S, D = hb.shape
    nb, W, _ = wb.shape
    yspec = pl.BlockSpec((tm, W), lambda n, i: (i, 0))
    return pl.pallas_call(
        _merge_kernel,
        grid=(D // tn, S // tm),
        in_specs=[pl.BlockSpec((tm, D), lambda n, i: (i, 0)), yspec, yspec, yspec, yspec,
                  pl.BlockSpec((nb, D, tn), lambda n, i: (0, 0, n)),
                  pl.BlockSpec((nb, W, tn), lambda n, i: (0, 0, n))],
        out_specs=pl.BlockSpec((tm, tn), lambda n, i: (i, n)),
        out_shape=jax.ShapeDtypeStruct((S, D), BF16),
        compiler_params=_cparams(("parallel", "arbitrary")),
        name="merge",
    )(hb, *ys, wg, wb)


def _outproj_kernel(m_ref, h_ref, w_ref, g_ref, b_ref, o_ref, *, alpha):
    y = alpha * h_ref[...] + _dot(m_ref[...], w_ref[...])
    o_ref[...] = _layer_norm(y, g_ref[...], b_ref[...])


def _outproj(merged, h, w, g, b, *, alpha, tm):
    S, D = h.shape
    return pl.pallas_call(
        functools.partial(_outproj_kernel, alpha=alpha),
        grid=(S // tm,),
        in_specs=[pl.BlockSpec((tm, D), lambda i: (i, 0)), pl.BlockSpec((tm, D), lambda i: (i, 0)),
                  pl.BlockSpec((D, D), lambda i: (0, 0)),
                  pl.BlockSpec((1, D), lambda i: (0, 0)), pl.BlockSpec((1, D), lambda i: (0, 0))],
        out_specs=pl.BlockSpec((tm, D), lambda i: (i, 0)),
        out_shape=jax.ShapeDtypeStruct((S, D), F32),
        compiler_params=_cparams(("parallel",)),
        name="outproj",
    )(merged, h, w, g.reshape(1, D), b.reshape(1, D))


def _swa_head_order():
    g = SWA_HEADS // SWA_KV_HEADS
    order = []
    for s in range(g):
        order += [s, g + s]
    return order


def _swa_col_perm():
    return np.concatenate([np.arange(h * HEAD_DIM, (h + 1) * HEAD_DIM) for h in _swa_head_order()])


def _rope_tables(S):
    half = HEAD_DIM // 2
    inv = ROPE_THETA ** (-jnp.arange(half, dtype=F32) / half)
    ang = jnp.arange(S, dtype=jnp.int32).astype(F32)[:, None] * inv[None, :]
    cos, sin = jnp.cos(ang), jnp.sin(ang)
    reps = LANES // HEAD_DIM
    cos_t = jnp.tile(jnp.concatenate([cos, cos], axis=1), (1, reps))
    sin_t = jnp.tile(jnp.concatenate([-sin, sin], axis=1), (1, reps))
    return cos_t, sin_t


def _pick(n, prefs):
    for p in prefs:
        if n % p == 0:
            return p
    return n


def kernel(x, ln_g, ln_b, ffn1_w_in, ffn1_w_out, w_in, pool_w, pool_scale, swa_sinks, mlstm_conv, mlstm_i_bias, mlstm_f_bias, fox_f_bias, w_branch, w_gate, w_out, ffn2_w_in, ffn2_w_out):
    B, S, D = x.shape
    depth = ln_g.shape[0]
    assert B == 1 and w_in.shape[2] == 4880 and S % ML_CHUNK == 0
    alpha = float((2 * depth) ** 0.25)
    perm = _swa_col_perm()
    cos_t, sin_t = _rope_tables(S)
    tm = _pick(S, (512, 256))
    t_mix = _pick(S, (512, 256))
    tq = _pick(S, (512, 256))

    h = x.reshape(S, D)
    for l in range(depth):
        wz = w_in[l]
        wz = jnp.concatenate([
            wz[:, 0:512], wz[:, 512:1024][:, perm], wz[:, 1280:3328], wz[:, 3336:4872], wz[:, 1024:1280],
            wz[:, 3328:3336], wz[:, 4872:4880], jnp.zeros((D, Z_COLS - COL_GATES - N_GATES), wz.dtype)],
            axis=1).astype(BF16)
        wb = w_branch[l].at[1].set(w_branch[l, 1][perm, :]).astype(BF16)
        wg = w_gate[l].astype(BF16)
        wo = w_out[l].astype(BF16)
        gate_bias = jnp.concatenate([mlstm_i_bias[l], mlstm_f_bias[l], fox_f_bias[l]]).reshape(1, N_GATES)
        sinks = swa_sinks[l]

        h, hb = _ffn(h, ffn1_w_in[l].astype(BF16), ffn1_w_out[l].astype(BF16), ln_g[l, 0], ln_b[l, 0],
                     alpha=alpha, tm=tm, tf=512)
        z, zb = _inproj(hb, wz, tm=tm, tn=Z_COLS // 3)
        gates_col = _gates(z, gate_bias, T=ML_CHUNK)
        gates_row = gates_col.T
        nh = FOX_HEADS // 2
        c_col = gates_col[:, 2 * MLSTM_HEADS:].reshape(S, nh, 2).transpose(1, 0, 2)
        c_row = gates_row[2 * MLSTM_HEADS:].reshape(nh, 2, S)

        ya = _pool(z, pool_w[l].astype(BF16), pool_scale[l], T=t_mix)
        yb = _swa(z, sinks, cos_t, sin_t, T=t_mix)
        yc = _mlstm(z, mlstm_conv[l], gates_col, gates_row, L=ML_CHUNK)
        yd = _fox(z, zb, c_col, c_row, t=tq)
        merged = _merge(hb, (ya, yb, yc, yd), wg, wb, tm=tm, tn=256)
        h = _outproj(merged, h, wo, ln_g[l, 1], ln_b[l, 1], alpha=alpha, tm=tm)
        h, hb = _ffn(h, ffn2_w_in[l].astype(BF16), ffn2_w_out[l].astype(BF16), ln_g[l, 2], ln_b[l, 2],
                     alpha=alpha, tm=tm, tf=512)
    return h.reshape(B, S, D)
```

```python
import functools

import jax
import jax.numpy as jnp
import numpy as np
from jax import lax
from jax.experimental import pallas as pl
from jax.experimental.pallas import tpu as pltpu

F32 = jnp.float32
BF16 = jnp.bfloat16

MIX_WIDTH = 512
HEAD_DIM = 64
POOL_WINDOWS = (2, 4, 8, 16)
POOL_CH = 128
SWA_HEADS = 8
SWA_KV_HEADS = 2
SWA_BLOCK = 128
ROPE_THETA = 10000.0
MLSTM_HEADS = 4
MLSTM_HEAD_DIM = 128
MLSTM_CONV = 4
FOX_HEADS = 8
LN_EPS = 1e-5
FFN_HALF = 0.5

LANES = 128
VMEM_LIMIT = 56 * 1024 * 1024

COL_POOL = 0
COL_SWA_Q = 512
COL_ML_Q = 1024
COL_ML_K = 1536
COL_ML_V = 2048
COL_ML_O = 2560
COL_FOX_Q = 3072
COL_FOX_K = 3584
COL_FOX_V = 4096
COL_SWA_K = 4608
COL_SWA_V = 4736
COL_GATES = 4864
Z_COLS = 4992
N_GATES = 16

ML_CHUNK = 256


def _cparams(sem):
    return pltpu.CompilerParams(dimension_semantics=sem, vmem_limit_bytes=VMEM_LIMIT)


def _layer_norm(y, g, b):
    mu = jnp.mean(y, axis=-1, keepdims=True)
    d = y - mu
    var = jnp.mean(d * d, axis=-1, keepdims=True)
    return d * lax.rsqrt(var + LN_EPS) * g + b


def _dot(a, b):
    return jnp.dot(a, b, preferred_element_type=F32)


def _dot_nt(a, b):
    return lax.dot_general(a, b, (((1,), (1,)), ((), ())), preferred_element_type=F32)


def _dot_tn(a, b):
    return lax.dot_general(a, b, (((0,), (0,)), ((), ())), preferred_element_type=F32)


def _ffn_kernel(x_ref, wg_ref, wu_ref, wo_ref, g_ref, b_ref, o_ref, ob_ref, xb_ref, acc_ref, *, alpha, nf):
    f = pl.program_id(1)

    @pl.when(f == 0)
    def _():
        xb_ref[...] = x_ref[...].astype(BF16)
        acc_ref[...] = jnp.zeros_like(acc_ref)

    xb = xb_ref[...]
    gate = _dot(xb, wg_ref[...])
    up = _dot(xb, wu_ref[...])
    act = (gate * jax.nn.sigmoid(gate) * up).astype(BF16)
    acc_ref[...] += _dot(act, wo_ref[...])

    @pl.when(f == nf - 1)
    def _():
        y = alpha * x_ref[...] + FFN_HALF * acc_ref[...]
        o = _layer_norm(y, g_ref[...], b_ref[...])
        o_ref[...] = o
        ob_ref[...] = o.astype(BF16)


def _ffn(x, w_in, w_out, g, b, *, alpha, tm, tf):
    S, D = x.shape
    DF = w_out.shape[0]
    nf = DF // tf
    assert S % tm == 0 and DF % tf == 0
    return pl.pallas_call(
        functools.partial(_ffn_kernel, alpha=alpha, nf=nf),
        grid=(S // tm, nf),
        in_specs=[
            pl.BlockSpec((tm, D), lambda i, f: (i, 0)),
            pl.BlockSpec((D, tf), lambda i, f: (0, f)),
            pl.BlockSpec((D, tf), lambda i, f: (0, f + nf)),
            pl.BlockSpec((tf, D), lambda i, f: (f, 0)),
            pl.BlockSpec((1, D), lambda i, f: (0, 0)),
            pl.BlockSpec((1, D), lambda i, f: (0, 0)),
        ],
        out_specs=[pl.BlockSpec((tm, D), lambda i, f: (i, 0)), pl.BlockSpec((tm, D), lambda i, f: (i, 0))],
        out_shape=[jax.ShapeDtypeStruct((S, D), F32), jax.ShapeDtypeStruct((S, D), BF16)],
        scratch_shapes=[pltpu.VMEM((tm, D), BF16), pltpu.VMEM((tm, D), F32)],
        compiler_params=_cparams(("parallel", "arbitrary")),
        name="ffn",
    )(x, w_in, w_in, w_out, g.reshape(1, D), b.reshape(1, D))


def _inproj_kernel(x_ref, w_ref, o_ref, ob_ref):
    z = _dot(x_ref[...], w_ref[...])
    o_ref[...] = z
    ob_ref[...] = z.astype(BF16)


def _inproj(xb, w, *, tm, tn):
    S, D = xb.shape
    N = w.shape[1]
    assert S % tm == 0 and N % tn == 0
    ospec = pl.BlockSpec((tm, tn), lambda i, j: (i, j))
    return pl.pallas_call(
        _inproj_kernel,
        grid=(S // tm, N // tn),
        in_specs=[pl.BlockSpec((tm, D), lambda i, j: (i, 0)), pl.BlockSpec((D, tn), lambda i, j: (0, j))],
        out_specs=[ospec, ospec],
        out_shape=[jax.ShapeDtypeStruct((S, N), F32), jax.ShapeDtypeStruct((S, N), BF16)],
        compiler_params=_cparams(("parallel", "arbitrary")),
        name="inproj",
    )(xb, w)


def _split3(x):
    hi = x.astype(BF16)
    r1 = x - hi.astype(F32)
    mid = r1.astype(BF16)
    lo = (r1 - mid.astype(F32)).astype(BF16)
    return hi, mid, lo


def _gates_kernel(z_ref, bias_ref, o_ref, carry_ref):
    c = pl.program_id(0)
    T = z_ref.shape[0]

    @pl.when(c == 0)
    def _():
        carry_ref[...] = jnp.zeros_like(carry_ref)

    pre = z_ref[:, 0:N_GATES] + bias_ref[...]
    lf = jax.nn.log_sigmoid(pre)
    row = lax.broadcasted_iota(jnp.int32, (T, T), 0)
    col = lax.broadcasted_iota(jnp.int32, (T, T), 1)
    tri = jnp.where(col <= row, 1.0, 0.0).astype(BF16)
    hi, mid, lo = _split3(lf)
    cs = _dot(tri, hi) + _dot(tri, mid) + _dot(tri, lo)
    lane = lax.broadcasted_iota(jnp.int32, (T, N_GATES), 1)
    is_fox = lane >= 2 * MLSTM_HEADS
    cs = cs + jnp.where(is_fox, carry_ref[...], 0.0)
    carry_ref[...] = cs[T - 1:T, :]
    o_ref[...] = jnp.where(lane < MLSTM_HEADS, pre, cs)


def _gates(z, bias, *, T):
    S = z.shape[0]
    return pl.pallas_call(
        _gates_kernel,
        grid=(S // T,),
        in_specs=[pl.BlockSpec((T, LANES), lambda c: (c, COL_GATES // LANES)),
                  pl.BlockSpec((1, N_GATES), lambda c: (0, 0))],
        out_specs=pl.BlockSpec((T, N_GATES), lambda c: (c, 0)),
        out_shape=jax.ShapeDtypeStruct((S, N_GATES), F32),
        scratch_shapes=[pltpu.VMEM((1, N_GATES), F32)],
        compiler_params=_cparams(("arbitrary",)),
        name="gates",
    )(z, bias)


POOL_HALO = 16


def _pool_kernel(u_ref, h_ref, w_ref, s_ref, o_ref):
    i = pl.program_id(0)
    T = u_ref.shape[0]
    u = u_ref[...]
    halo = jnp.where(i > 0, h_ref[...], 0.0)
    full = jnp.concatenate([halo, u], axis=0)
    t = i * T + lax.broadcasted_iota(jnp.int32, (T, 1), 0) + 1
    outs = []
    for g, win in enumerate(POOL_WINDOWS):
        cols = slice(g * POOL_CH, (g + 1) * POOL_CH)
        fg = full[:, cols]
        acc = fg[POOL_HALO:POOL_HALO + T]
        for j in range(1, win):
            acc = acc + fg[POOL_HALO - j:POOL_HALO - j + T]
        count = jnp.minimum(t, win).astype(F32)
        pooled = acc / count - u[:, cols]
        outs.append(_dot(pooled.astype(BF16), w_ref[g]))
    y = jnp.concatenate(outs, axis=1) * s_ref[...]
    o_ref[...] = y.astype(BF16)


def _pool(z, w, scale, *, T):
    S = z.shape[0]
    r = T // POOL_HALO
    return pl.pallas_call(
        _pool_kernel,
        grid=(S // T,),
        in_specs=[
            pl.BlockSpec((T, MIX_WIDTH), lambda i: (i, 0)),
            pl.BlockSpec((POOL_HALO, MIX_WIDTH), lambda i: (jnp.maximum(i * r - 1, 0), 0)),
            pl.BlockSpec((len(POOL_WINDOWS), POOL_CH, POOL_CH), lambda i: (0, 0, 0)),
            pl.BlockSpec((1, MIX_WIDTH), lambda i: (0, 0)),
        ],
        out_specs=pl.BlockSpec((T, MIX_WIDTH), lambda i: (i, 0)),
        out_shape=jax.ShapeDtypeStruct((S, MIX_WIDTH), BF16),
        compiler_params=_cparams(("parallel",)),
        name="pool",
    )(z, z, w, scale.reshape(1, MIX_WIDTH))


def _rope(x, cos, sin_signed):
    n = x.shape[1] // LANES
    lane = lax.broadcasted_iota(jnp.int32, (x.shape[0], LANES), 1)
    first_half = (lane % HEAD_DIM) < (HEAD_DIM // 2)
    outs = []
    for s in range(n):
        xs = x[:, s * LANES:(s + 1) * LANES]
        partner = jnp.where(first_half, pltpu.roll(xs, LANES - HEAD_DIM // 2, axis=1),
                            pltpu.roll(xs, HEAD_DIM // 2, axis=1))
        outs.append(xs * cos + partner * sin_signed)
    return outs[0] if n == 1 else jnp.concatenate(outs, axis=1)


def _swa_kernel(sink_ref, q_ref, k_ref, v_ref, kh_ref, vh_ref, cos_ref, sin_ref, cosh_ref, sinh_ref, o_ref, *, nblk):
    i = pl.program_id(0)
    B = SWA_BLOCK
    G = SWA_HEADS // SWA_KV_HEADS
    cos = cos_ref[...]
    sin = sin_ref[...]
    q = (_rope(q_ref[...], cos, sin) * (HEAD_DIM ** -0.5)).astype(BF16)
    k = jnp.concatenate([_rope(kh_ref[...], cosh_ref[...], sinh_ref[...]), _rope(k_ref[...], cos, sin)],
                        axis=0).astype(BF16)
    v = jnp.concatenate([vh_ref[...], v_ref[...]], axis=0).astype(BF16)

    lane = lax.broadcasted_iota(jnp.int32, (G * B, LANES), 1)
    low = lane < HEAD_DIM
    rowq = lax.broadcasted_iota(jnp.int32, (G * B, 2 * B), 0) % B
    colk = lax.broadcasted_iota(jnp.int32, (G * B, 2 * B), 1)
    band = (colk > rowq) & (colk <= rowq + B)
    grp = lax.broadcasted_iota(jnp.int32, (G * B, 1), 0) // B
    sinks = []
    for kvh in range(SWA_KV_HEADS):
        col = jnp.zeros((G * B, 1), F32)
        for g in range(G):
            col = jnp.where(grp == g, sink_ref[kvh * G + g], col)
        sinks.append(col)

    for blk in range(nblk):
        r0 = blk * B
        qb = jnp.concatenate([q[r0:r0 + B, s * LANES:(s + 1) * LANES] for s in range(G)], axis=0)
        kb = k[r0:r0 + 2 * B]
        vb = v[r0:r0 + 2 * B]
        valid = band
        if blk == 0:
            valid = band & ((colk >= B) | (i > 0))
        o_halves = []
        for kvh in range(SWA_KV_HEADS):
            qm = jnp.where(low if kvh == 0 else ~low, qb, jnp.zeros_like(qb))
            s = _dot_nt(qm, kb)
            s = jnp.where(valid, s, -jnp.inf)
            m = jnp.maximum(jnp.max(s, axis=1, keepdims=True), sinks[kvh])
            p = jnp.exp(s - m)
            den = jnp.sum(p, axis=1, keepdims=True) + jnp.exp(sinks[kvh] - m)
            o_halves.append(_dot(p.astype(BF16), vb) / den)
        o = jnp.where(low, o_halves[0], o_halves[1])
        for s_ in range(G):
            o_ref[r0:r0 + B, s_ * LANES:(s_ + 1) * LANES] = o[s_ * B:(s_ + 1) * B].astype(BF16)


def _swa(z, sinks, cos, sin, *, T):
    S = z.shape[0]
    nblk = T // SWA_BLOCK
    cq, ck, cv = COL_SWA_Q // MIX_WIDTH, COL_SWA_K // LANES, COL_SWA_V // LANES
    halo = lambda i: jnp.maximum(i * nblk - 1, 0)
    return pl.pallas_call(
        functools.partial(_swa_kernel, nblk=nblk),
        grid=(S // T,),
        in_specs=[
            pl.BlockSpec(memory_space=pltpu.SMEM),
            pl.BlockSpec((T, MIX_WIDTH), lambda i: (i, cq)),
            pl.BlockSpec((T, LANES), lambda i: (i, ck)),
            pl.BlockSpec((T, LANES), lambda i: (i, cv)),
            pl.BlockSpec((SWA_BLOCK, LANES), lambda i: (halo(i), ck)),
            pl.BlockSpec((SWA_BLOCK, LANES), lambda i: (halo(i), cv)),
            pl.BlockSpec((T, LANES), lambda i: (i, 0)),
            pl.BlockSpec((T, LANES), lambda i: (i, 0)),
            pl.BlockSpec((SWA_BLOCK, LANES), lambda i: (halo(i), 0)),
            pl.BlockSpec((SWA_BLOCK, LANES), lambda i: (halo(i), 0)),
        ],
        out_specs=pl.BlockSpec((T, MIX_WIDTH), lambda i: (i, 0)),
        out_shape=jax.ShapeDtypeStruct((S, MIX_WIDTH), BF16),
        compiler_params=_cparams(("parallel",)),
        name="swa",
    )(sinks, z, z, z, z, z, cos, sin, cos, sin)


CONV_HALO = 8


def _mlstm_kernel(q_ref, k_ref, v_ref, og_ref, qh_ref, kh_ref, cw_ref, gc_ref, gr_ref, o_ref, c_sc, n_sc, m_sc):
    c = pl.program_id(0)
    L = q_ref.shape[0]
    H, dh = MLSTM_HEADS, MLSTM_HEAD_DIM

    @pl.when(c == 0)
    def _():
        c_sc[...] = jnp.zeros_like(c_sc)
        n_sc[...] = jnp.zeros_like(n_sc)
        m_sc[...] = jnp.zeros_like(m_sc)

    def conv_silu(x_ref, halo_ref, w):
        halo = jnp.where(c > 0, halo_ref[...], 0.0)
        full = jnp.concatenate([halo, x_ref[...]], axis=0)
        acc = jnp.zeros((L, x_ref.shape[1]), F32)
        for j in range(MLSTM_CONV):
            off = CONV_HALO - (MLSTM_CONV - 1) + j
            acc = acc + w[j:j + 1, :] * full[off:off + L]
        return acc * jax.nn.sigmoid(acc)

    cw = cw_ref[...]
    qc = conv_silu(q_ref, qh_ref, cw[:, :H * dh]) * (dh ** -0.5)
    kc = conv_silu(k_ref, kh_ref, cw[:, H * dh:])
    vv = v_ref[...]
    gc = gc_ref[...]
    gr = gr_ref[...]
    row = lax.broadcasted_iota(jnp.int32, (L, L), 0)
    col = lax.broadcasted_iota(jnp.int32, (L, L), 1)
    causal = col <= row

    for h in range(H):
        sl = slice(h * dh, (h + 1) * dh)
        qf = qc[:, sl]
        qh, kh, vh = qf.astype(BF16), kc[:, sl].astype(BF16), vv[:, sl].astype(BF16)
        i_col = gc[:, h:h + 1]
        b_col = gc[:, H + h:H + h + 1]
        a_row = gr[h:h + 1, :] - gr[H + h:H + h + 1, :]
        m_prev = m_sc[h][0:1, 0:1]
        c_prev = c_sc[h]
        n_prev = n_sc[h]

        dlog = jnp.where(causal, b_col + a_row, -jnp.inf)
        inter = b_col + m_prev
        m_comb = jnp.maximum(inter, jnp.max(dlog, axis=1, keepdims=True))
        sm = _dot_nt(qh, kh) * jnp.exp(dlog - m_comb)
        w_inter = jnp.exp(inter - m_comb)
        num = _dot(sm.astype(BF16), vh) + w_inter * _dot_nt(qh, c_prev.astype(BF16))
        den = jnp.sum(sm, axis=1, keepdims=True) + w_inter * jnp.sum(qf * n_prev, axis=1, keepdims=True)
        hid = num / jnp.maximum(jnp.abs(den), jnp.exp(-m_comb))
        o_ref[:, sl] = (jax.nn.sigmoid(og_ref[:, sl]) * hid).astype(BF16)

        bl = b_col[L - 1:L, :]
        g = bl - b_col + i_col
        m_new = jnp.maximum(bl + m_prev, jnp.max(g, axis=0, keepdims=True))
        wk = jnp.exp(g - m_new)
        decay = jnp.exp(bl + m_prev - m_new)
        c_sc[h] = decay * c_prev + _dot_tn((wk * vv[:, sl]).astype(BF16), kh)
        n_sc[h] = decay * n_prev + jnp.sum(wk * kc[:, sl], axis=0, keepdims=True)
        m_sc[h] = jnp.broadcast_to(m_new, m_sc.shape[1:])


def _mlstm(z, conv_w, gates_col, gates_row, *, L):
    S = z.shape[0]
    r = L // CONV_HALO
    W = MLSTM_HEADS * MLSTM_HEAD_DIM
    halo = lambda c: jnp.maximum(c * r - 1, 0)
    return pl.pallas_call(
        _mlstm_kernel,
        grid=(S // L,),
        in_specs=[
            pl.BlockSpec((L, W), lambda c: (c, COL_ML_Q // W)),
            pl.BlockSpec((L, W), lambda c: (c, COL_ML_K // W)),
            pl.BlockSpec((L, W), lambda c: (c, COL_ML_V // W)),
            pl.BlockSpec((L, W), lambda c: (c, COL_ML_O // W)),
            pl.BlockSpec((CONV_HALO, W), lambda c: (halo(c), COL_ML_Q // W)),
            pl.BlockSpec((CONV_HALO, W), lambda c: (halo(c), COL_ML_K // W)),
            pl.BlockSpec((MLSTM_CONV, 2 * W), lambda c: (0, 0)),
            pl.BlockSpec((L, N_GATES), lambda c: (c, 0)),
            pl.BlockSpec((N_GATES, L), lambda c: (0, c)),
        ],
        out_specs=pl.BlockSpec((L, W), lambda c: (c, 0)),
        out_shape=jax.ShapeDtypeStruct((S, W), BF16),
        scratch_shapes=[pltpu.VMEM((MLSTM_HEADS, MLSTM_HEAD_DIM, MLSTM_HEAD_DIM), F32),
                        pltpu.VMEM((MLSTM_HEADS, 1, MLSTM_HEAD_DIM), F32),
                        pltpu.VMEM((MLSTM_HEADS, 8, LANES), F32)],
        compiler_params=_cparams(("arbitrary",)),
        name="mlstm",
    )(z, z, z, z, z, z, conv_w, gates_col, gates_row)


LOG2E = 1.4426950408889634


def _fox_kernel(q_ref, k_ref, v_ref, cq_ref, ck_ref, o_ref, m_sc, acc_sc, s_sc, *, t):
    i = pl.program_id(1)
    nch = t // LANES
    m_sc[...] = jnp.full_like(m_sc, -jnp.inf)
    acc_sc[...] = jnp.zeros_like(acc_sc)

    q = q_ref[...] * (LOG2E * HEAD_DIM ** -0.5)
    lane = lax.broadcasted_iota(jnp.int32, (t, LANES), 1)
    row = lax.broadcasted_iota(jnp.int32, (t, LANES), 0)
    low = lane < HEAD_DIM
    own = [low, ~low]
    qa = [jnp.where(own[a], q, 0.0).astype(BF16) for a in range(2)]
    cq = cq_ref[...] * LOG2E
    ca = [jnp.broadcast_to(cq[:, a:a + 1], (t, LANES)) for a in range(2)]

    def scores(j, slot):
        j0 = pl.multiple_of(j * t, t)
        kb = k_ref[pl.ds(j0, t), :]
        for a in range(2):
            s_sc[slot, a] = _dot_nt(qa[a], kb)

    def softmax_pv(j, slot, masked):
        j0 = pl.multiple_of(j * t, t)
        vb = v_ref[pl.ds(j0, t), :]
        ck = ck_ref[:, pl.ds(j0, t)] * LOG2E
        ps, alphas = [], []
        for a in range(2):
            sc = []
            for c in range(nch):
                x = s_sc[slot, a, :, c * LANES:(c + 1) * LANES] - ck[a:a + 1, c * LANES:(c + 1) * LANES]
                if masked:
                    x = jnp.where(lane + c * LANES <= row, x, -jnp.inf)
                sc.append(x)
            mx = sc[0]
            for c in range(1, nch):
                mx = jnp.maximum(mx, sc[c])
            m_old = m_sc[a]
            m_new = jnp.maximum(m_old, jnp.max(mx, axis=1, keepdims=True) + ca[a])
            alpha = jnp.exp2(m_old - m_new)
            mm = m_new - ca[a]
            ps += [jnp.exp2(x - mm).astype(BF16) for x in sc]
            alphas.append(alpha)
            m_sc[a] = m_new
        one, zero = jnp.ones_like(vb), jnp.zeros_like(vb)
        vblk = jnp.concatenate([jnp.concatenate([jnp.where(low, vb, one), zero], axis=1),
                                jnp.concatenate([zero, jnp.where(low, one, vb)], axis=1)], axis=0)
        acc_sc[...] = jnp.concatenate(alphas, axis=1) * acc_sc[...] + _dot(jnp.concatenate(ps, axis=1), vblk)

    scores(0, 0)

    def body(j2, carry):
        j = 2 * j2
        scores(j + 1, 1)
        softmax_pv(j, 0, False)
        scores(j + 2, 0)
        softmax_pv(j + 1, 1, False)
        return carry

    lax.fori_loop(0, i // 2, body, 0)

    @pl.when(i % 2 == 1)
    def _():
        scores(i, 1)
        softmax_pv(i - 1, 0, False)
        softmax_pv(i, 1, True)

    @pl.when(i % 2 == 0)
    def _():
        softmax_pv(i, 0, True)

    outs = []
    for a in range(2):
        acc = acc_sc[:, a * LANES:(a + 1) * LANES]
        outs.append(acc / pltpu.roll(acc, HEAD_DIM, axis=1))
    o_ref[...] = jnp.where(low, outs[0], outs[1]).astype(BF16)


def _fox(z, zb, c_col, c_row, *, t):
    S = z.shape[0]
    npair = FOX_HEADS // 2
    cq0, ck0, cv0 = COL_FOX_Q // LANES, COL_FOX_K // LANES, COL_FOX_V // LANES
    return pl.pallas_call(
        functools.partial(_fox_kernel, t=t),
        grid=(npair, S // t),
        in_specs=[
            pl.BlockSpec((t, LANES), lambda p, i: (i, cq0 + p)),
            pl.BlockSpec((S, LANES), lambda p, i: (0, ck0 + p)),
            pl.BlockSpec((S, LANES), lambda p, i: (0, cv0 + p)),
            pl.BlockSpec((None, t, 2), lambda p, i: (p, i, 0)),
            pl.BlockSpec((None, 2, S), lambda p, i: (p, 0, 0)),
        ],
        out_specs=pl.BlockSpec((t, LANES), lambda p, i: (i, p)),
        out_shape=jax.ShapeDtypeStruct((S, MIX_WIDTH), BF16),
        scratch_shapes=[pltpu.VMEM((2, t, LANES), F32), pltpu.VMEM((t, 2 * LANES), F32),
                        pltpu.VMEM((2, 2, t, t), F32)],
        compiler_params=_cparams(("parallel", "arbitrary")),
        name="fox",
    )(z, zb, zb, c_col, c_row)


def _merge_kernel(hb_ref, ya_ref, yb_ref, yc_ref, yd_ref, wg_ref, wb_ref, o_ref):
    hb = hb_ref[...]
    acc = None
    for b, y_ref in enumerate((ya_ref, yb_ref, yc_ref, yd_ref)):
        gate = jax.nn.sigmoid(_dot(hb, wg_ref[b]))
        term = gate * _dot(y_ref[...], wb_ref[b])
        acc = term if acc is None else acc + term
    o_ref[...] = acc.astype(BF16)


def _merge(hb, ys, wg, wb, *, tm, tn):
    S, D = hb.shape
    nb, W, _ = wb.shape
    yspec = pl.BlockSpec((tm, W), lambda n, i: (i, 0))
    return pl.pallas_call(
        _merge_kernel,
        grid=(D // tn, S // tm),
        in_specs=[pl.BlockSpec((tm, D), lambda n, i: (i, 0)), yspec, yspec, yspec, yspec,
                  pl.BlockSpec((nb, D, tn), lambda n, i: (0, 0, n)),
                  pl.BlockSpec((nb, W, tn), lambda n, i: (0, 0, n))],
        out_specs=pl.BlockSpec((tm, tn), lambda n, i: (i, n)),
        out_shape=jax.ShapeDtypeStruct((S, D), BF16),
        compiler_params=_cparams(("parallel", "arbitrary")),
        name="merge",
    )(hb, *ys, wg, wb)


def _outproj_kernel(m_ref, h_ref, w_ref, g_ref, b_ref, o_ref, *, alpha):
    y = alpha * h_ref[...] + _dot(m_ref[...], w_ref[...])
    o_ref[...] = _layer_norm(y, g_ref[...], b_ref[...])


def _outproj(merged, h, w, g, b, *, alpha, tm):
    S, D = h.shape
    return pl.pallas_call(
        functools.partial(_outproj_kernel, alpha=alpha),
        grid=(S // tm,),
        in_specs=[pl.BlockSpec((tm, D), lambda i: (i, 0)), pl.BlockSpec((tm, D), lambda i: (i, 0)),
                  pl.BlockSpec((D, D), lambda i: (0, 0)),
                  pl.BlockSpec((1, D), lambda i: (0, 0)), pl.BlockSpec((1, D), lambda i: (0, 0))],
        out_specs=pl.BlockSpec((tm, D), lambda i: (i, 0)),
        out_shape=jax.ShapeDtypeStruct((S, D), F32),
        compiler_params=_cparams(("parallel",)),
        name="outproj",
    )(merged, h, w, g.reshape(1, D), b.reshape(1, D))


def _swa_head_order():
    g = SWA_HEADS // SWA_KV_HEADS
    order = []
    for s in range(g):
        order += [s, g + s]
    return order


def _swa_col_perm():
    return np.concatenate([np.arange(h * HEAD_DIM, (h + 1) * HEAD_DIM) for h in _swa_head_order()])


def _rope_tables(S):
    half = HEAD_DIM // 2
    inv = ROPE_THETA ** (-jnp.arange(half, dtype=F32) / half)
    ang = jnp.arange(S, dtype=jnp.int32).astype(F32)[:, None] * inv[None, :]
    cos, sin = jnp.cos(ang), jnp.sin(ang)
    reps = LANES // HEAD_DIM
    cos_t = jnp.tile(jnp.concatenate([cos, cos], axis=1), (1, reps))
    sin_t = jnp.tile(jnp.concatenate([-sin, sin], axis=1), (1, reps))
    return cos_t, sin_t


def _pick(n, prefs):
    for p in prefs:
        if n % p == 0:
            return p
    return n


def kernel(x, ln_g, ln_b, ffn1_w_in, ffn1_w_out, w_in, pool_w, pool_scale, swa_sinks, mlstm_conv, mlstm_i_bias, mlstm_f_bias, fox_f_bias, w_branch, w_gate, w_out, ffn2_w_in, ffn2_w_out):
    B, S, D = x.shape
    depth = ln_g.shape[0]
    assert B == 1 and w_in.shape[2] == 4880 and S % ML_CHUNK == 0
    alpha = float((2 * depth) ** 0.25)
    perm = _swa_col_perm()
    cos_t, sin_t = _rope_tables(S)
    tm = _pick(S, (512, 256))
    t_mix = _pick(S, (512, 256))
    tq = _pick(S, (512, 256))

    h = x.reshape(S, D)
    for l in range(depth):
        wz = w_in[l]
        wz = jnp.concatenate([
            wz[:, 0:512], wz[:, 512:1024][:, perm], wz[:, 1280:3328], wz[:, 3336:4872], wz[:, 1024:1280],
            wz[:, 3328:3336], wz[:, 4872:4880], jnp.zeros((D, Z_COLS - COL_GATES - N_GATES), wz.dtype)],
            axis=1).astype(BF16)
        wb = w_branch[l].at[1].set(w_branch[l, 1][perm, :]).astype(BF16)
        wg = w_gate[l].astype(BF16)
        wo = w_out[l].astype(BF16)
        gate_bias = jnp.concatenate([mlstm_i_bias[l], mlstm_f_bias[l], fox_f_bias[l]]).reshape(1, N_GATES)
        sinks = swa_sinks[l]

        h, hb = _ffn(h, ffn1_w_in[l].astype(BF16), ffn1_w_out[l].astype(BF16), ln_g[l, 0], ln_b[l, 0],
                     alpha=alpha, tm=tm, tf=512)
        z, zb = _inproj(hb, wz, tm=tm, tn=Z_COLS // 3)
        gates_col = _gates(z, gate_bias, T=ML_CHUNK)
        gates_row = gates_col.T
        nh = FOX_HEADS // 2
        c_col = gates_col[:, 2 * MLSTM_HEADS:].reshape(S, nh, 2).transpose(1, 0, 2)
        c_row = gates_row[2 * MLSTM_HEADS:].reshape(nh, 2, S)

        ya = _pool(z, pool_w[l].astype(BF16), pool_scale[l], T=t_mix)
        yb = _swa(z, sinks, cos_t, sin_t, T=t_mix)
        yc = _mlstm(z, mlstm_conv[l], gates_col, gates_row, L=ML_CHUNK)
        yd = _fox(z, zb, c_col, c_row, t=tq)
        merged = _merge(hb, (ya, yb, yc, yd), wg, wb, tm=_pick(S, (1024, 512, 256)), tn=512)
        h = _outproj(merged, h, wo, ln_g[l, 1], ln_b[l, 1], alpha=alpha, tm=tm)
        h, hb = _ffn(h, ffn2_w_in[l].astype(BF16), ffn2_w_out[l].astype(BF16), ln_g[l, 2], ln_b[l, 2],
                     alpha=alpha, tm=tm, tf=512)
    return h.reshape(B, S, D)
```

```python
import functools

import jax
import jax.numpy as jnp
import numpy as np
from jax import lax
from jax.experimental import pallas as pl
from jax.experimental.pallas import tpu as pltpu

F32 = jnp.float32
BF16 = jnp.bfloat16

MIX_WIDTH = 512
HEAD_DIM = 64
POOL_WINDOWS = (2, 4, 8, 16)
POOL_CH = 128
SWA_HEADS = 8
SWA_KV_HEADS = 2
SWA_BLOCK = 128
ROPE_THETA = 10000.0
MLSTM_HEADS = 4
MLSTM_HEAD_DIM = 128
MLSTM_CONV = 4
FOX_HEADS = 8
LN_EPS = 1e-5
FFN_HALF = 0.5
LOG2E = 1.4426950408889634

LANES = 128
VMEM_LIMIT = 60 * 1024 * 1024

COL_POOL = 0
COL_SWA_Q = 512
COL_ML_Q = 1024
COL_ML_K = 1536
COL_ML_V = 2048
COL_ML_O = 2560
COL_FOX_Q = 3072
COL_FOX_K = 3584
COL_FOX_V = 4096
COL_SWA_K = 4608
COL_SWA_V = 4736
COL_GATES = 4864
Z_COLS = 5120
N_GATES = 16

ML_CHUNK = 256


def _cparams(sem):
    return pltpu.CompilerParams(dimension_semantics=sem, vmem_limit_bytes=VMEM_LIMIT)


def _layer_norm(y, g, b):
    mu = jnp.mean(y, axis=-1, keepdims=True)
    d = y - mu
    var = jnp.mean(d * d, axis=-1, keepdims=True)
    return d * lax.rsqrt(var + LN_EPS) * g + b


def _dot(a, b):
    return jnp.dot(a, b, preferred_element_type=F32)


def _dot_nt(a, b):
    return lax.dot_general(a, b, (((1,), (1,)), ((), ())), preferred_element_type=F32)


def _dot_tn(a, b):
    return lax.dot_general(a, b, (((0,), (0,)), ((), ())), preferred_element_type=F32)


def _ffn_kernel(x_ref, wg_ref, wu_ref, wo_ref, g_ref, b_ref, o_ref, ob_ref, xb_ref, acc_ref, *, alpha, nf):
    f = pl.program_id(1)

    @pl.when(f == 0)
    def _():
        xb_ref[...] = x_ref[...].astype(BF16)
        acc_ref[...] = jnp.zeros_like(acc_ref)

    half = xb_ref.shape[0] // 2
    for r in range(2):
        rs = slice(r * half, (r + 1) * half)
        xb = xb_ref[rs, :]
        gate = _dot(xb, wg_ref[...])
        up = _dot(xb, wu_ref[...])
        act = (gate * jax.nn.sigmoid(gate) * up).astype(BF16)
        nw = acc_ref.shape[1] // 2
        for n in range(2):
            cols = slice(n * nw, (n + 1) * nw)
            acc_ref[rs, cols] += _dot(act, wo_ref[:, cols])

    @pl.when(f == nf - 1)
    def _():
        g, b = g_ref[...], b_ref[...]
        rows = 64

        def chunk(r, carry):
            rs = pl.ds(pl.multiple_of(r * rows, rows), rows)
            y = alpha * x_ref[rs, :] + FFN_HALF * acc_ref[rs, :]
            o = _layer_norm(y, g, b)
            o_ref[rs, :] = o
            ob_ref[rs, :] = o.astype(BF16)
            return carry

        lax.fori_loop(0, x_ref.shape[0] // rows, chunk, 0)


def _ffn(x, w_in, w_out, layer, g, b, *, alpha, tm, tf):
    S, D = x.shape
    DF = w_out.shape[1]
    nf = DF // tf
    assert S % tm == 0 and DF % tf == 0
    return pl.pallas_call(
        functools.partial(_ffn_kernel, alpha=alpha, nf=nf),
        grid=(S // tm, nf),
        in_specs=[
            pl.BlockSpec((tm, D), lambda i, f: (i, 0)),
            pl.BlockSpec((None, D, tf), lambda i, f: (layer, 0, f)),
            pl.BlockSpec((None, D, tf), lambda i, f: (layer, 0, f + nf)),
            pl.BlockSpec((None, tf, D), lambda i, f: (layer, f, 0)),
            pl.BlockSpec((1, D), lambda i, f: (0, 0)),
            pl.BlockSpec((1, D), lambda i, f: (0, 0)),
        ],
        out_specs=[pl.BlockSpec((tm, D), lambda i, f: (i, 0), pipeline_mode=pl.Buffered(1)),
                   pl.BlockSpec((tm, D), lambda i, f: (i, 0), pipeline_mode=pl.Buffered(1))],
        out_shape=[jax.ShapeDtypeStruct((S, D), F32), jax.ShapeDtypeStruct((S, D), BF16)],
        scratch_shapes=[pltpu.VMEM((tm, D), BF16), pltpu.VMEM((tm, D), F32)],
        compiler_params=_cparams(("parallel", "arbitrary")),
        name="ffn",
    )(x, w_in, w_in, w_out, g.reshape(1, D), b.reshape(1, D))


def _inproj_kernel(x_ref, w_ref, o_ref, ob_ref):
    z = _dot(x_ref[...], w_ref[...])
    o_ref[...] = z
    ob_ref[...] = z.astype(BF16)


def _inproj(xb, w, layer, *, tm, tn):
    S, D = xb.shape
    N = w.shape[2]
    assert S % tm == 0 and N % tn == 0
    ospec = pl.BlockSpec((tm, tn), lambda i, j: (i, j))
    return pl.pallas_call(
        _inproj_kernel,
        grid=(S // tm, N // tn),
        in_specs=[pl.BlockSpec((tm, D), lambda i, j: (i, 0)),
                  pl.BlockSpec((None, D, tn), lambda i, j: (layer, 0, j))],
        out_specs=[ospec, ospec],
        out_shape=[jax.ShapeDtypeStruct((S, N), F32), jax.ShapeDtypeStruct((S, N), BF16)],
        compiler_params=_cparams(("parallel", "arbitrary")),
        name="inproj",
    )(xb, w)


def _split3(x):
    hi = x.astype(BF16)
    r1 = x - hi.astype(F32)
    mid = r1.astype(BF16)
    lo = (r1 - mid.astype(F32)).astype(BF16)
    return hi, mid, lo


def _gates_kernel(z_ref, bias_ref, o_ref, carry_ref):
    c = pl.program_id(0)
    T = z_ref.shape[0]

    @pl.when(c == 0)
    def _():
        carry_ref[...] = jnp.zeros_like(carry_ref)

    pre = z_ref[:, 0:N_GATES] + bias_ref[...]
    lf = jax.nn.log_sigmoid(pre)
    row = lax.broadcasted_iota(jnp.int32, (T, T), 0)
    col = lax.broadcasted_iota(jnp.int32, (T, T), 1)
    tri = jnp.where(col <= row, 1.0, 0.0).astype(BF16)
    hi, mid, lo = _split3(lf)
    cs = _dot(tri, hi) + _dot(tri, mid) + _dot(tri, lo)
    lane = lax.broadcasted_iota(jnp.int32, (T, N_GATES), 1)
    is_fox = lane >= 2 * MLSTM_HEADS
    cs = cs + jnp.where(is_fox, carry_ref[...], 0.0)
    carry_ref[...] = cs[T - 1:T, :]
    o_ref[...] = jnp.where(lane < MLSTM_HEADS, pre, cs)


def _gates(z, bias, *, T):
    S = z.shape[0]
    return pl.pallas_call(
        _gates_kernel,
        grid=(S // T,),
        in_specs=[pl.BlockSpec((T, LANES), lambda c: (c, COL_GATES // LANES)),
                  pl.BlockSpec((1, N_GATES), lambda c: (0, 0))],
        out_specs=pl.BlockSpec((T, N_GATES), lambda c: (c, 0)),
        out_shape=jax.ShapeDtypeStruct((S, N_GATES), F32),
        scratch_shapes=[pltpu.VMEM((1, N_GATES), F32)],
        compiler_params=_cparams(("arbitrary",)),
        name="gates",
    )(z, bias)


POOL_HALO = 16


def _pool_kernel(u_ref, h_ref, w_ref, s_ref, o_ref):
    i = pl.program_id(0)
    T = u_ref.shape[0]
    u = u_ref[...]
    halo = jnp.where(i > 0, h_ref[...], 0.0)
    full = jnp.concatenate([halo, u], axis=0)
    t = i * T + lax.broadcasted_iota(jnp.int32, (T, 1), 0) + 1
    outs = []
    for g, win in enumerate(POOL_WINDOWS):
        cols = slice(g * POOL_CH, (g + 1) * POOL_CH)
        fg = full[:, cols]
        acc = fg[POOL_HALO:POOL_HALO + T]
        for j in range(1, win):
            acc = acc + fg[POOL_HALO - j:POOL_HALO - j + T]
        count = jnp.minimum(t, win).astype(F32)
        pooled = acc / count - u[:, cols]
        outs.append(_dot(pooled.astype(BF16), w_ref[g]))
    y = jnp.concatenate(outs, axis=1) * s_ref[...]
    o_ref[...] = y.astype(BF16)


def _pool(z, w, scale, *, T):
    S = z.shape[0]
    r = T // POOL_HALO
    return pl.pallas_call(
        _pool_kernel,
        grid=(S // T,),
        in_specs=[
            pl.BlockSpec((T, MIX_WIDTH), lambda i: (i, 0)),
            pl.BlockSpec((POOL_HALO, MIX_WIDTH), lambda i: (jnp.maximum(i * r - 1, 0), 0)),
            pl.BlockSpec((len(POOL_WINDOWS), POOL_CH, POOL_CH), lambda i: (0, 0, 0)),
            pl.BlockSpec((1, MIX_WIDTH), lambda i: (0, 0)),
        ],
        out_specs=pl.BlockSpec((T, MIX_WIDTH), lambda i: (i, 0)),
        out_shape=jax.ShapeDtypeStruct((S, MIX_WIDTH), BF16),
        compiler_params=_cparams(("parallel",)),
        name="pool",
    )(z, z, w, scale.reshape(1, MIX_WIDTH))


def _rope(x, cos, sin_signed):
    n = x.shape[1] // LANES
    lane = lax.broadcasted_iota(jnp.int32, (x.shape[0], LANES), 1)
    first_half = (lane % HEAD_DIM) < (HEAD_DIM // 2)
    outs = []
    for s in range(n):
        xs = x[:, s * LANES:(s + 1) * LANES]
        partner = jnp.where(first_half, pltpu.roll(xs, LANES - HEAD_DIM // 2, axis=1),
                            pltpu.roll(xs, HEAD_DIM // 2, axis=1))
        outs.append(xs * cos + partner * sin_signed)
    return outs[0] if n == 1 else jnp.concatenate(outs, axis=1)


def _swa_kernel(sink_ref, q_ref, k_ref, v_ref, kh_ref, vh_ref, cos_ref, sin_ref, cosh_ref, sinh_ref, o_ref, *, nblk):
    i = pl.program_id(0)
    B = SWA_BLOCK
    G = SWA_HEADS // SWA_KV_HEADS
    cos = cos_ref[...]
    sin = sin_ref[...]
    q = (_rope(q_ref[...], cos, sin) * (HEAD_DIM ** -0.5)).astype(BF16)
    k = jnp.concatenate([_rope(kh_ref[...], cosh_ref[...], sinh_ref[...]), _rope(k_ref[...], cos, sin)],
                        axis=0).astype(BF16)
    v = jnp.concatenate([vh_ref[...], v_ref[...]], axis=0).astype(BF16)

    lane = lax.broadcasted_iota(jnp.int32, (G * B, LANES), 1)
    low = lane < HEAD_DIM
    rowq = lax.broadcasted_iota(jnp.int32, (G * B, 2 * B), 0) % B
    colk = lax.broadcasted_iota(jnp.int32, (G * B, 2 * B), 1)
    band = (colk > rowq) & (colk <= rowq + B)
    grp = lax.broadcasted_iota(jnp.int32, (G * B, 1), 0) // B
    sinks = []
    for kvh in range(SWA_KV_HEADS):
        col = jnp.zeros((G * B, 1), F32)
        for g in range(G):
            col = jnp.where(grp == g, sink_ref[kvh * G + g], col)
        sinks.append(col)

    for blk in range(nblk):
        r0 = blk * B
        qb = jnp.concatenate([q[r0:r0 + B, s * LANES:(s + 1) * LANES] for s in range(G)], axis=0)
        kb = k[r0:r0 + 2 * B]
        vb = v[r0:r0 + 2 * B]
        valid = band
        if blk == 0:
            valid = band & ((colk >= B) | (i > 0))
        o_halves = []
        for kvh in range(SWA_KV_HEADS):
            qm = jnp.where(low if kvh == 0 else ~low, qb, jnp.zeros_like(qb))
            s = _dot_nt(qm, kb)
            s = jnp.where(valid, s, -jnp.inf)
            m = jnp.maximum(jnp.max(s, axis=1, keepdims=True), sinks[kvh])
            p = jnp.exp(s - m)
            den = jnp.sum(p, axis=1, keepdims=True) + jnp.exp(sinks[kvh] - m)
            o_halves.append(_dot(p.astype(BF16), vb) / den)
        o = jnp.where(low, o_halves[0], o_halves[1])
        for s_ in range(G):
            o_ref[r0:r0 + B, s_ * LANES:(s_ + 1) * LANES] = o[s_ * B:(s_ + 1) * B].astype(BF16)


def _swa(z, sinks, cos, sin, *, T):
    S = z.shape[0]
    nblk = T // SWA_BLOCK
    cq, ck, cv = COL_SWA_Q // MIX_WIDTH, COL_SWA_K // LANES, COL_SWA_V // LANES
    halo = lambda i: jnp.maximum(i * nblk - 1, 0)
    return pl.pallas_call(
        functools.partial(_swa_kernel, nblk=nblk),
        grid=(S // T,),
        in_specs=[
            pl.BlockSpec(memory_space=pltpu.SMEM),
            pl.BlockSpec((T, MIX_WIDTH), lambda i: (i, cq)),
            pl.BlockSpec((T, LANES), lambda i: (i, ck)),
            pl.BlockSpec((T, LANES), lambda i: (i, cv)),
            pl.BlockSpec((SWA_BLOCK, LANES), lambda i: (halo(i), ck)),
            pl.BlockSpec((SWA_BLOCK, LANES), lambda i: (halo(i), cv)),
            pl.BlockSpec((T, LANES), lambda i: (i, 0)),
            pl.BlockSpec((T, LANES), lambda i: (i, 0)),
            pl.BlockSpec((SWA_BLOCK, LANES), lambda i: (halo(i), 0)),
            pl.BlockSpec((SWA_BLOCK, LANES), lambda i: (halo(i), 0)),
        ],
        out_specs=pl.BlockSpec((T, MIX_WIDTH), lambda i: (i, 0)),
        out_shape=jax.ShapeDtypeStruct((S, MIX_WIDTH), BF16),
        compiler_params=_cparams(("parallel",)),
        name="swa",
    )(sinks, z, z, z, z, z, cos, sin, cos, sin)


CONV_HALO = 8


def _mlstm_kernel(q_ref, k_ref, v_ref, og_ref, qh_ref, kh_ref, cw_ref, gc_ref, gr_ref, o_ref, c_sc, n_sc, m_sc):
    c = pl.program_id(0)
    L = q_ref.shape[0]
    H, dh = MLSTM_HEADS, MLSTM_HEAD_DIM

    @pl.when(c == 0)
    def _():
        c_sc[...] = jnp.zeros_like(c_sc)
        n_sc[...] = jnp.zeros_like(n_sc)
        m_sc[...] = jnp.zeros_like(m_sc)

    def conv_silu(x_ref, halo_ref, w):
        halo = jnp.where(c > 0, halo_ref[...], 0.0)
        full = jnp.concatenate([halo, x_ref[...]], axis=0)
        acc = jnp.zeros((L, x_ref.shape[1]), F32)
        for j in range(MLSTM_CONV):
            off = CONV_HALO - (MLSTM_CONV - 1) + j
            acc = acc + w[j:j + 1, :] * full[off:off + L]
        return acc * jax.nn.sigmoid(acc)

    cw = cw_ref[...]
    qc = conv_silu(q_ref, qh_ref, cw[:, :H * dh]) * (dh ** -0.5)
    kc = conv_silu(k_ref, kh_ref, cw[:, H * dh:])
    vv = v_ref[...]
    gc = gc_ref[...]
    gr = gr_ref[...]
    row = lax.broadcasted_iota(jnp.int32, (L, L), 0)
    col = lax.broadcasted_iota(jnp.int32, (L, L), 1)
    causal = col <= row

    for h in range(H):
        sl = slice(h * dh, (h + 1) * dh)
        qf = qc[:, sl]
        qh, kh, vh = qf.astype(BF16), kc[:, sl].astype(BF16), vv[:, sl].astype(BF16)
        i_col = gc[:, h:h + 1]
        b_col = gc[:, H + h:H + h + 1]
        a_row = gr[h:h + 1, :] - gr[H + h:H + h + 1, :]
        m_prev = m_sc[h][0:1, 0:1]
        c_prev = c_sc[h]
        n_prev = n_sc[h]

        dlog = jnp.where(causal, b_col + a_row, -jnp.inf)
        inter = b_col + m_prev
        m_comb = jnp.maximum(inter, jnp.max(dlog, axis=1, keepdims=True))
        sm = _dot_nt(qh, kh) * jnp.exp(dlog - m_comb)
        w_inter = jnp.exp(inter - m_comb)
        num = _dot(sm.astype(BF16), vh) + w_inter * _dot_nt(qh, c_prev.astype(BF16))
        den = jnp.sum(sm, axis=1, keepdims=True) + w_inter * jnp.sum(qf * n_prev, axis=1, keepdims=True)
        hid = num / jnp.maximum(jnp.abs(den), jnp.exp(-m_comb))
        o_ref[:, sl] = (jax.nn.sigmoid(og_ref[:, sl]) * hid).astype(BF16)

        bl = b_col[L - 1:L, :]
        g = bl - b_col + i_col
        m_new = jnp.maximum(bl + m_prev, jnp.max(g, axis=0, keepdims=True))
        wk = jnp.exp(g - m_new)
        decay = jnp.exp(bl + m_prev - m_new)
        c_sc[h] = decay * c_prev + _dot_tn((wk * vv[:, sl]).astype(BF16), kh)
        n_sc[h] = decay * n_prev + jnp.sum(wk * kc[:, sl], axis=0, keepdims=True)
        m_sc[h] = jnp.broadcast_to(m_new, m_sc.shape[1:])


def _mlstm(z, conv_w, gates_col, gates_row, *, L):
    S = z.shape[0]
    r = L // CONV_HALO
    W = MLSTM_HEADS * MLSTM_HEAD_DIM
    halo = lambda c: jnp.maximum(c * r - 1, 0)
    return pl.pallas_call(
        _mlstm_kernel,
        grid=(S // L,),
        in_specs=[
            pl.BlockSpec((L, W), lambda c: (c, COL_ML_Q // W)),
            pl.BlockSpec((L, W), lambda c: (c, COL_ML_K // W)),
            pl.BlockSpec((L, W), lambda c: (c, COL_ML_V // W)),
            pl.BlockSpec((L, W), lambda c: (c, COL_ML_O // W)),
            pl.BlockSpec((CONV_HALO, W), lambda c: (halo(c), COL_ML_Q // W)),
            pl.BlockSpec((CONV_HALO, W), lambda c: (halo(c), COL_ML_K // W)),
            pl.BlockSpec((MLSTM_CONV, 2 * W), lambda c: (0, 0)),
            pl.BlockSpec((L, N_GATES), lambda c: (c, 0)),
            pl.BlockSpec((N_GATES, L), lambda c: (0, c)),
        ],
        out_specs=pl.BlockSpec((L, W), lambda c: (c, 0)),
        out_shape=jax.ShapeDtypeStruct((S, W), BF16),
        scratch_shapes=[pltpu.VMEM((MLSTM_HEADS, MLSTM_HEAD_DIM, MLSTM_HEAD_DIM), F32),
                        pltpu.VMEM((MLSTM_HEADS, 1, MLSTM_HEAD_DIM), F32),
                        pltpu.VMEM((MLSTM_HEADS, 8, LANES), F32)],
        compiler_params=_cparams(("arbitrary",)),
        name="mlstm",
    )(z, z, z, z, z, z, conv_w, gates_col, gates_row)


def _fox_kernel(q_ref, k_ref, v_ref, cq_ref, ck_ref, o_ref, m_sc, acc_sc, s_sc, *, t):
    i = pl.program_id(1)
    nch = t // LANES
    m_sc[...] = jnp.full_like(m_sc, -jnp.inf)
    acc_sc[...] = jnp.zeros_like(acc_sc)

    q = q_ref[...] * (LOG2E * HEAD_DIM ** -0.5)
    lane = lax.broadcasted_iota(jnp.int32, (t, LANES), 1)
    row = lax.broadcasted_iota(jnp.int32, (t, LANES), 0)
    low = lane < HEAD_DIM
    own = [low, ~low]
    qa = [jnp.where(own[a], q, 0.0).astype(BF16) for a in range(2)]
    cq = cq_ref[...] * LOG2E
    ca = [jnp.broadcast_to(cq[:, a:a + 1], (t, LANES)) for a in range(2)]

    def scores(j, slot):
        j0 = pl.multiple_of(j * t, t)
        kb = k_ref[pl.ds(j0, t), :]
        for a in range(2):
            s_sc[slot, a] = _dot_nt(qa[a], kb)

    def softmax_pv(j, slot, masked):
        j0 = pl.multiple_of(j * t, t)
        vb = v_ref[pl.ds(j0, t), :]
        ck = ck_ref[:, pl.ds(j0, t)] * LOG2E
        ps, alphas = [], []
        for a in range(2):
            sc = []
            for c in range(nch):
                x = s_sc[slot, a, :, c * LANES:(c + 1) * LANES] - ck[a:a + 1, c * LANES:(c + 1) * LANES]
                if masked:
                    x = jnp.where(lane + c * LANES <= row, x, -jnp.inf)
                sc.append(x)
            mx = sc[0]
            for c in range(1, nch):
                mx = jnp.maximum(mx, sc[c])
            m_old = m_sc[a]
            m_new = jnp.maximum(m_old, jnp.max(mx, axis=1, keepdims=True) + ca[a])
            alpha = jnp.exp2(m_old - m_new)
            mm = m_new - ca[a]
            ps += [jnp.exp2(x - mm).astype(BF16) for x in sc]
            alphas.append(alpha)
            m_sc[a] = m_new
        one, zero = jnp.ones_like(vb), jnp.zeros_like(vb)
        vblk = jnp.concatenate([jnp.concatenate([jnp.where(low, vb, one), zero], axis=1),
                                jnp.concatenate([zero, jnp.where(low, one, vb)], axis=1)], axis=0)
        acc_sc[...] = jnp.concatenate(alphas, axis=1) * acc_sc[...] + _dot(jnp.concatenate(ps, axis=1), vblk)

    scores(0, 0)

    def body(j2, carry):
        j = 2 * j2
        scores(j + 1, 1)
        softmax_pv(j, 0, False)
        scores(j + 2, 0)
        softmax_pv(j + 1, 1, False)
        return carry

    lax.fori_loop(0, i // 2, body, 0)

    @pl.when(i % 2 == 1)
    def _():
        scores(i, 1)
        softmax_pv(i - 1, 0, False)
        softmax_pv(i, 1, True)

    @pl.when(i % 2 == 0)
    def _():
        softmax_pv(i, 0, True)

    outs = []
    for a in range(2):
        acc = acc_sc[:, a * LANES:(a + 1) * LANES]
        outs.append(acc / pltpu.roll(acc, HEAD_DIM, axis=1))
    o_ref[...] = jnp.where(low, outs[0], outs[1]).astype(BF16)


def _fox(z, zb, c_col, c_row, *, t):
    S = z.shape[0]
    npair = FOX_HEADS // 2
    cq0, ck0, cv0 = COL_FOX_Q // LANES, COL_FOX_K // LANES, COL_FOX_V // LANES
    return pl.pallas_call(
        functools.partial(_fox_kernel, t=t),
        grid=(npair, S // t),
        in_specs=[
            pl.BlockSpec((t, LANES), lambda p, i: (i, cq0 + p)),
            pl.BlockSpec((S, LANES), lambda p, i: (0, ck0 + p)),
            pl.BlockSpec((S, LANES), lambda p, i: (0, cv0 + p)),
            pl.BlockSpec((None, t, 2), lambda p, i: (p, i, 0)),
            pl.BlockSpec((None, 2, S), lambda p, i: (p, 0, 0)),
        ],
        out_specs=pl.BlockSpec((t, LANES), lambda p, i: (i, p)),
        out_shape=jax.ShapeDtypeStruct((S, MIX_WIDTH), BF16),
        scratch_shapes=[pltpu.VMEM((2, t, LANES), F32), pltpu.VMEM((t, 2 * LANES), F32),
                        pltpu.VMEM((2, 2, t, t), F32)],
        compiler_params=_cparams(("parallel", "arbitrary")),
        name="fox",
    )(z, zb, zb, c_col, c_row)


def _merge_kernel(hb_ref, ya_ref, yb_ref, yc_ref, yd_ref, wg_ref, wb_ref, o_ref):
    hb = hb_ref[...]
    acc = None
    for b, y_ref in enumerate((ya_ref, yb_ref, yc_ref, yd_ref)):
        gate = jax.nn.sigmoid(_dot(hb, wg_ref[b]))
        term = gate * _dot(y_ref[...], wb_ref[b])
        acc = term if acc is None else acc + term
    o_ref[...] = acc.astype(BF16)


def _merge(hb, ys, wg, wb, layer, *, tm, tn):
    S, D = hb.shape
    _, nb, W, _ = wb.shape
    yspec = pl.BlockSpec((tm, W), lambda n, i: (i, 0))
    return pl.pallas_call(
        _merge_kernel,
        grid=(D // tn, S // tm),
        in_specs=[pl.BlockSpec((tm, D), lambda n, i: (i, 0)), yspec, yspec, yspec, yspec,
                  pl.BlockSpec((None, nb, D, tn), lambda n, i: (layer, 0, 0, n)),
                  pl.BlockSpec((None, nb, W, tn), lambda n, i: (layer, 0, 0, n))],
        out_specs=pl.BlockSpec((tm, tn), lambda n, i: (i, n)),
        out_shape=jax.ShapeDtypeStruct((S, D), BF16),
        compiler_params=_cparams(("parallel", "arbitrary")),
        name="merge",
    )(hb, *ys, wg, wb)


def _outproj_kernel(m_ref, h_ref, w_ref, g_ref, b_ref, o_ref, *, alpha):
    y = alpha * h_ref[...] + _dot(m_ref[...], w_ref[...])
    o_ref[...] = _layer_norm(y, g_ref[...], b_ref[...])


def _outproj(merged, h, w, layer, g, b, *, alpha, tm):
    S, D = h.shape
    return pl.pallas_call(
        functools.partial(_outproj_kernel, alpha=alpha),
        grid=(S // tm,),
        in_specs=[pl.BlockSpec((tm, D), lambda i: (i, 0)), pl.BlockSpec((tm, D), lambda i: (i, 0)),
                  pl.BlockSpec((None, D, D), lambda i: (layer, 0, 0)),
                  pl.BlockSpec((1, D), lambda i: (0, 0)), pl.BlockSpec((1, D), lambda i: (0, 0))],
        out_specs=pl.BlockSpec((tm, D), lambda i: (i, 0)),
        out_shape=jax.ShapeDtypeStruct((S, D), F32),
        compiler_params=_cparams(("parallel",)),
        name="outproj",
    )(merged, h, w, g.reshape(1, D), b.reshape(1, D))


def _swa_head_order():
    g = SWA_HEADS // SWA_KV_HEADS
    order = []
    for s in range(g):
        order += [s, g + s]
    return order


def _swa_col_perm():
    return np.concatenate([np.arange(h * HEAD_DIM, (h + 1) * HEAD_DIM) for h in _swa_head_order()])


def _rope_tables(S):
    half = HEAD_DIM // 2
    inv = ROPE_THETA ** (-jnp.arange(half, dtype=F32) / half)
    ang = jnp.arange(S, dtype=jnp.int32).astype(F32)[:, None] * inv[None, :]
    cos, sin = jnp.cos(ang), jnp.sin(ang)
    reps = LANES // HEAD_DIM
    cos_t = jnp.tile(jnp.concatenate([cos, cos], axis=1), (1, reps))
    sin_t = jnp.tile(jnp.concatenate([-sin, sin], axis=1), (1, reps))
    return cos_t, sin_t


def _pick(n, prefs):
    for p in prefs:
        if n % p == 0:
            return p
    return n


def kernel(x, ln_g, ln_b, ffn1_w_in, ffn1_w_out, w_in, pool_w, pool_scale, swa_sinks, mlstm_conv, mlstm_i_bias, mlstm_f_bias, fox_f_bias, w_branch, w_gate, w_out, ffn2_w_in, ffn2_w_out):
    B, S, D = x.shape
    depth = ln_g.shape[0]
    assert B == 1 and w_in.shape[2] == 4880 and S % ML_CHUNK == 0
    alpha = float((2 * depth) ** 0.25)
    perm = _swa_col_perm()
    cos_t, sin_t = _rope_tables(S)
    tm = _pick(S, (512, 256))
    tm_ffn = _pick(S, (1024, 512, 256))
    t_mix = _pick(S, (512, 256))
    tq = _pick(S, (512, 256))

    wz = jnp.concatenate([
        w_in[..., 0:512], w_in[..., 512:1024][..., perm], w_in[..., 1280:3328], w_in[..., 3336:4872],
        w_in[..., 1024:1280], w_in[..., 3328:3336], w_in[..., 4872:4880],
        jnp.zeros((depth, D, Z_COLS - COL_GATES - N_GATES), w_in.dtype)], axis=2).astype(BF16)
    wb = w_branch.at[:, 1].set(w_branch[:, 1][:, perm, :]).astype(BF16)
    wg = w_gate.astype(BF16)
    wo = w_out.astype(BF16)
    f1_in, f1_out = ffn1_w_in.astype(BF16), ffn1_w_out.astype(BF16)
    f2_in, f2_out = ffn2_w_in.astype(BF16), ffn2_w_out.astype(BF16)

    h = x.reshape(S, D)
    for l in range(depth):
        gate_bias = jnp.concatenate([mlstm_i_bias[l], mlstm_f_bias[l], fox_f_bias[l]]).reshape(1, N_GATES)
        sinks = swa_sinks[l]

        h, hb = _ffn(h, f1_in, f1_out, l, ln_g[l, 0], ln_b[l, 0], alpha=alpha, tm=tm_ffn, tf=512)
        z, zb = _inproj(hb, wz, l, tm=tm, tn=Z_COLS // 2)
        gates_col = _gates(z, gate_bias, T=ML_CHUNK)
        gates_row = gates_col.T
        nh = FOX_HEADS // 2
        c_col = gates_col[:, 2 * MLSTM_HEADS:].reshape(S, nh, 2).transpose(1, 0, 2)

        ya = _pool(z, pool_w[l].astype(BF16), pool_scale[l], T=t_mix)
        yb = _swa(z, sinks, cos_t, sin_t, T=t_mix)
        yc = _mlstm(z, mlstm_conv[l], gates_col, gates_row, L=ML_CHUNK)
        c_row = gates_row[2 * MLSTM_HEADS:].reshape(nh, 2, S)
        yd = _fox(z, zb, c_col, c_row, t=tq)
        merged = _merge(hb, (ya, yb, yc, yd), wg, wb, l, tm=_pick(S, (1024, 512, 256)), tn=512)
        h = _outproj(merged, h, wo, l, ln_g[l, 1], ln_b[l, 1], alpha=alpha, tm=tm)
        h, hb = _ffn(h, f2_in, f2_out, l, ln_g[l, 2], ln_b[l, 2], alpha=alpha, tm=tm_ffn, tf=512)
    return h.reshape(B, S, D)
```

```python
import functools

import jax
import jax.numpy as jnp
import numpy as np
from jax import lax
from jax.experimental import pallas as pl
from jax.experimental.pallas import tpu as pltpu

F32 = jnp.float32
BF16 = jnp.bfloat16

MIX_WIDTH = 512
HEAD_DIM = 64
POOL_WINDOWS = (2, 4, 8, 16)
POOL_CH = 128
SWA_HEADS = 8
SWA_KV_HEADS = 2
SWA_BLOCK = 128
ROPE_THETA = 10000.0
MLSTM_HEADS = 4
MLSTM_HEAD_DIM = 128
MLSTM_CONV = 4
FOX_HEADS = 8
LN_EPS = 1e-5
FFN_HALF = 0.5
LOG2E = 1.4426950408889634

LANES = 128
VMEM_LIMIT = 60 * 1024 * 1024

COL_POOL = 0
COL_SWA_Q = 512
COL_ML_Q = 1024
COL_ML_K = 1536
COL_ML_V = 2048
COL_ML_O = 2560
COL_FOX_Q = 3072
COL_FOX_K = 3584
COL_FOX_V = 4096
COL_SWA_K = 4608
COL_SWA_V = 4736
COL_GATES = 4864
Z_COLS = 5120
N_GATES = 16

ML_CHUNK = 256


def _cparams(sem):
    return pltpu.CompilerParams(dimension_semantics=sem, vmem_limit_bytes=VMEM_LIMIT)


def _layer_norm(y, g, b):
    mu = jnp.mean(y, axis=-1, keepdims=True)
    d = y - mu
    var = jnp.mean(d * d, axis=-1, keepdims=True)
    return d * lax.rsqrt(var + LN_EPS) * g + b


def _dot(a, b):
    return jnp.dot(a, b, preferred_element_type=F32)


def _dot_nt(a, b):
    return lax.dot_general(a, b, (((1,), (1,)), ((), ())), preferred_element_type=F32)


def _dot_tn(a, b):
    return lax.dot_general(a, b, (((0,), (0,)), ((), ())), preferred_element_type=F32)


def _ffn_kernel(x_ref, wg_ref, wu_ref, wo_ref, g_ref, b_ref, o_ref, ob_ref, xb_ref, acc_ref, *, alpha, nf):
    f = pl.program_id(1)

    @pl.when(f == 0)
    def _():
        xb_ref[...] = x_ref[...].astype(BF16)
        acc_ref[...] = jnp.zeros_like(acc_ref)

    half = xb_ref.shape[0] // 2
    for r in range(2):
        rs = slice(r * half, (r + 1) * half)
        xb = xb_ref[rs, :]
        gate = _dot(xb, wg_ref[...])
        up = _dot(xb, wu_ref[...])
        act = (gate * jax.nn.sigmoid(gate) * up).astype(BF16)
        nw = acc_ref.shape[1] // 2
        for n in range(2):
            cols = slice(n * nw, (n + 1) * nw)
            acc_ref[rs, cols] += _dot(act, wo_ref[:, cols])

    @pl.when(f == nf - 1)
    def _():
        g, b = g_ref[...], b_ref[...]
        rows = 64

        def chunk(r, carry):
            rs = pl.ds(pl.multiple_of(r * rows, rows), rows)
            y = alpha * x_ref[rs, :] + FFN_HALF * acc_ref[rs, :]
            o = _layer_norm(y, g, b)
            o_ref[rs, :] = o
            ob_ref[rs, :] = o.astype(BF16)
            return carry

        lax.fori_loop(0, x_ref.shape[0] // rows, chunk, 0)


def _ffn(x, w_in, w_out, layer, g, b, *, alpha, tm, tf):
    S, D = x.shape
    DF = w_out.shape[1]
    nf = DF // tf
    assert S % tm == 0 and DF % tf == 0
    return pl.pallas_call(
        functools.partial(_ffn_kernel, alpha=alpha, nf=nf),
        grid=(S // tm, nf),
        in_specs=[
            pl.BlockSpec((tm, D), lambda i, f: (i, 0)),
            pl.BlockSpec((None, D, tf), lambda i, f: (layer, 0, f)),
            pl.BlockSpec((None, D, tf), lambda i, f: (layer, 0, f + nf)),
            pl.BlockSpec((None, tf, D), lambda i, f: (layer, f, 0)),
            pl.BlockSpec((1, D), lambda i, f: (0, 0)),
            pl.BlockSpec((1, D), lambda i, f: (0, 0)),
        ],
        out_specs=[pl.BlockSpec((tm, D), lambda i, f: (i, 0), pipeline_mode=pl.Buffered(1)),
                   pl.BlockSpec((tm, D), lambda i, f: (i, 0), pipeline_mode=pl.Buffered(1))],
        out_shape=[jax.ShapeDtypeStruct((S, D), F32), jax.ShapeDtypeStruct((S, D), BF16)],
        scratch_shapes=[pltpu.VMEM((tm, D), BF16), pltpu.VMEM((tm, D), F32)],
        compiler_params=_cparams(("parallel", "arbitrary")),
        name="ffn",
    )(x, w_in, w_in, w_out, g.reshape(1, D), b.reshape(1, D))


def _inproj_kernel(x_ref, w_ref, o_ref, ob_ref):
    z = _dot(x_ref[...], w_ref[...])
    o_ref[...] = z
    ob_ref[...] = z.astype(BF16)


def _inproj(xb, w, layer, *, tm, tn):
    S, D = xb.shape
    N = w.shape[2]
    assert S % tm == 0 and N % tn == 0
    ospec = pl.BlockSpec((tm, tn), lambda j, i: (i, j))
    return pl.pallas_call(
        _inproj_kernel,
        grid=(N // tn, S // tm),
        in_specs=[pl.BlockSpec((tm, D), lambda j, i: (i, 0)),
                  pl.BlockSpec((None, D, tn), lambda j, i: (layer, 0, j))],
        out_specs=[ospec, ospec],
        out_shape=[jax.ShapeDtypeStruct((S, N), F32), jax.ShapeDtypeStruct((S, N), BF16)],
        compiler_params=_cparams(("parallel", "arbitrary")),
        name="inproj",
    )(xb, w)


def _split3(x):
    hi = x.astype(BF16)
    r1 = x - hi.astype(F32)
    mid = r1.astype(BF16)
    lo = (r1 - mid.astype(F32)).astype(BF16)
    return hi, mid, lo


def _gates_kernel(z_ref, bias_ref, o_ref, carry_ref):
    c = pl.program_id(0)
    T = z_ref.shape[0]

    @pl.when(c == 0)
    def _():
        carry_ref[...] = jnp.zeros_like(carry_ref)

    pre = z_ref[:, 0:N_GATES] + bias_ref[...]
    lf = jax.nn.log_sigmoid(pre)
    row = lax.broadcasted_iota(jnp.int32, (T, T), 0)
    col = lax.broadcasted_iota(jnp.int32, (T, T), 1)
    tri = jnp.where(col <= row, 1.0, 0.0).astype(BF16)
    hi, mid, lo = _split3(lf)
    cs = _dot(tri, hi) + _dot(tri, mid) + _dot(tri, lo)
    lane = lax.broadcasted_iota(jnp.int32, (T, N_GATES), 1)
    is_fox = lane >= 2 * MLSTM_HEADS
    cs = cs + jnp.where(is_fox, carry_ref[...], 0.0)
    carry_ref[...] = cs[T - 1:T, :]
    o_ref[...] = jnp.where(lane < MLSTM_HEADS, pre, cs)


def _gates(z, bias, *, T):
    S = z.shape[0]
    return pl.pallas_call(
        _gates_kernel,
        grid=(S // T,),
        in_specs=[pl.BlockSpec((T, LANES), lambda c: (c, COL_GATES // LANES)),
                  pl.BlockSpec((1, N_GATES), lambda c: (0, 0))],
        out_specs=pl.BlockSpec((T, N_GATES), lambda c: (c, 0)),
        out_shape=jax.ShapeDtypeStruct((S, N_GATES), F32),
        scratch_shapes=[pltpu.VMEM((1, N_GATES), F32)],
        compiler_params=_cparams(("arbitrary",)),
        name="gates",
    )(z, bias)


POOL_HALO = 16


def _pool_kernel(u_ref, h_ref, w_ref, s_ref, o_ref):
    i = pl.program_id(0)
    T = u_ref.shape[0]
    u = u_ref[...]
    halo = jnp.where(i > 0, h_ref[...], 0.0)
    full = jnp.concatenate([halo, u], axis=0)
    t = i * T + lax.broadcasted_iota(jnp.int32, (T, 1), 0) + 1
    outs = []
    for g, win in enumerate(POOL_WINDOWS):
        cols = slice(g * POOL_CH, (g + 1) * POOL_CH)
        fg = full[:, cols]
        acc = fg[POOL_HALO:POOL_HALO + T]
        for j in range(1, win):
            acc = acc + fg[POOL_HALO - j:POOL_HALO - j + T]
        count = jnp.minimum(t, win).astype(F32)
        pooled = acc / count - u[:, cols]
        outs.append(_dot(pooled.astype(BF16), w_ref[g]))
    y = jnp.concatenate(outs, axis=1) * s_ref[...]
    o_ref[...] = y.astype(BF16)


def _pool(z, w, scale, *, T):
    S = z.shape[0]
    r = T // POOL_HALO
    return pl.pallas_call(
        _pool_kernel,
        grid=(S // T,),
        in_specs=[
            pl.BlockSpec((T, MIX_WIDTH), lambda i: (i, 0)),
            pl.BlockSpec((POOL_HALO, MIX_WIDTH), lambda i: (jnp.maximum(i * r - 1, 0), 0)),
            pl.BlockSpec((len(POOL_WINDOWS), POOL_CH, POOL_CH), lambda i: (0, 0, 0)),
            pl.BlockSpec((1, MIX_WIDTH), lambda i: (0, 0)),
        ],
        out_specs=pl.BlockSpec((T, MIX_WIDTH), lambda i: (i, 0)),
        out_shape=jax.ShapeDtypeStruct((S, MIX_WIDTH), BF16),
        compiler_params=_cparams(("parallel",)),
        name="pool",
    )(z, z, w, scale.reshape(1, MIX_WIDTH))


def _rope(x, cos, sin_signed):
    n = x.shape[1] // LANES
    lane = lax.broadcasted_iota(jnp.int32, (x.shape[0], LANES), 1)
    first_half = (lane % HEAD_DIM) < (HEAD_DIM // 2)
    outs = []
    for s in range(n):
        xs = x[:, s * LANES:(s + 1) * LANES]
        partner = jnp.where(first_half, pltpu.roll(xs, LANES - HEAD_DIM // 2, axis=1),
                            pltpu.roll(xs, HEAD_DIM // 2, axis=1))
        outs.append(xs * cos + partner * sin_signed)
    return outs[0] if n == 1 else jnp.concatenate(outs, axis=1)


def _swa_kernel(sink_ref, q_ref, k_ref, v_ref, kh_ref, vh_ref, cos_ref, sin_ref, cosh_ref, sinh_ref, o_ref, *, nblk):
    i = pl.program_id(0)
    B = SWA_BLOCK
    G = SWA_HEADS // SWA_KV_HEADS
    cos = cos_ref[...]
    sin = sin_ref[...]
    q = (_rope(q_ref[...], cos, sin) * (HEAD_DIM ** -0.5)).astype(BF16)
    k = jnp.concatenate([_rope(kh_ref[...], cosh_ref[...], sinh_ref[...]), _rope(k_ref[...], cos, sin)],
                        axis=0).astype(BF16)
    v = jnp.concatenate([vh_ref[...], v_ref[...]], axis=0).astype(BF16)

    lane = lax.broadcasted_iota(jnp.int32, (G * B, LANES), 1)
    low = lane < HEAD_DIM
    rowq = lax.broadcasted_iota(jnp.int32, (G * B, 2 * B), 0) % B
    colk = lax.broadcasted_iota(jnp.int32, (G * B, 2 * B), 1)
    band = (colk > rowq) & (colk <= rowq + B)
    grp = lax.broadcasted_iota(jnp.int32, (G * B, 1), 0) // B
    sinks = []
    for kvh in range(SWA_KV_HEADS):
        col = jnp.zeros((G * B, 1), F32)
        for g in range(G):
            col = jnp.where(grp == g, sink_ref[kvh * G + g], col)
        sinks.append(col)

    for blk in range(nblk):
        r0 = blk * B
        qb = jnp.concatenate([q[r0:r0 + B, s * LANES:(s + 1) * LANES] for s in range(G)], axis=0)
        kb = k[r0:r0 + 2 * B]
        vb = v[r0:r0 + 2 * B]
        valid = band
        if blk == 0:
            valid = band & ((colk >= B) | (i > 0))
        o_halves = []
        for kvh in range(SWA_KV_HEADS):
            qm = jnp.where(low if kvh == 0 else ~low, qb, jnp.zeros_like(qb))
            s = _dot_nt(qm, kb)
            s = jnp.where(valid, s, -jnp.inf)
            m = jnp.maximum(jnp.max(s, axis=1, keepdims=True), sinks[kvh])
            p = jnp.exp(s - m)
            den = jnp.sum(p, axis=1, keepdims=True) + jnp.exp(sinks[kvh] - m)
            o_halves.append(_dot(p.astype(BF16), vb) / den)
        o = jnp.where(low, o_halves[0], o_halves[1])
        for s_ in range(G):
            o_ref[r0:r0 + B, s_ * LANES:(s_ + 1) * LANES] = o[s_ * B:(s_ + 1) * B].astype(BF16)


def _swa(z, sinks, cos, sin, *, T):
    S = z.shape[0]
    nblk = T // SWA_BLOCK
    cq, ck, cv = COL_SWA_Q // MIX_WIDTH, COL_SWA_K // LANES, COL_SWA_V // LANES
    halo = lambda i: jnp.maximum(i * nblk - 1, 0)
    return pl.pallas_call(
        functools.partial(_swa_kernel, nblk=nblk),
        grid=(S // T,),
        in_specs=[
            pl.BlockSpec(memory_space=pltpu.SMEM),
            pl.BlockSpec((T, MIX_WIDTH), lambda i: (i, cq)),
            pl.BlockSpec((T, LANES), lambda i: (i, ck)),
            pl.BlockSpec((T, LANES), lambda i: (i, cv)),
            pl.BlockSpec((SWA_BLOCK, LANES), lambda i: (halo(i), ck)),
            pl.BlockSpec((SWA_BLOCK, LANES), lambda i: (halo(i), cv)),
            pl.BlockSpec((T, LANES), lambda i: (i, 0)),
            pl.BlockSpec((T, LANES), lambda i: (i, 0)),
            pl.BlockSpec((SWA_BLOCK, LANES), lambda i: (halo(i), 0)),
            pl.BlockSpec((SWA_BLOCK, LANES), lambda i: (halo(i), 0)),
        ],
        out_specs=pl.BlockSpec((T, MIX_WIDTH), lambda i: (i, 0)),
        out_shape=jax.ShapeDtypeStruct((S, MIX_WIDTH), BF16),
        compiler_params=_cparams(("parallel",)),
        name="swa",
    )(sinks, z, z, z, z, z, cos, sin, cos, sin)


CONV_HALO = 8


def _mlstm_kernel(q_ref, k_ref, v_ref, og_ref, qh_ref, kh_ref, cw_ref, gc_ref, gr_ref, o_ref, c_sc, n_sc, m_sc):
    c = pl.program_id(0)
    L = q_ref.shape[0]
    H, dh = MLSTM_HEADS, MLSTM_HEAD_DIM

    @pl.when(c == 0)
    def _():
        c_sc[...] = jnp.zeros_like(c_sc)
        n_sc[...] = jnp.zeros_like(n_sc)
        m_sc[...] = jnp.zeros_like(m_sc)

    def conv_silu(x_ref, halo_ref, w):
        halo = jnp.where(c > 0, halo_ref[...], 0.0)
        full = jnp.concatenate([halo, x_ref[...]], axis=0)
        acc = jnp.zeros((L, x_ref.shape[1]), F32)
        for j in range(MLSTM_CONV):
            off = CONV_HALO - (MLSTM_CONV - 1) + j
            acc = acc + w[j:j + 1, :] * full[off:off + L]
        return acc * jax.nn.sigmoid(acc)

    cw = cw_ref[...]
    qc = conv_silu(q_ref, qh_ref, cw[:, :H * dh]) * (dh ** -0.5)
    kc = conv_silu(k_ref, kh_ref, cw[:, H * dh:])
    vv = v_ref[...]
    gc = gc_ref[...]
    gr = gr_ref[...]
    row = lax.broadcasted_iota(jnp.int32, (L, L), 0)
    col = lax.broadcasted_iota(jnp.int32, (L, L), 1)
    causal = col <= row

    for h in range(H):
        sl = slice(h * dh, (h + 1) * dh)
        qf = qc[:, sl]
        qh, kh, vh = qf.astype(BF16), kc[:, sl].astype(BF16), vv[:, sl].astype(BF16)
        i_col = gc[:, h:h + 1]
        b_col = gc[:, H + h:H + h + 1]
        a_row = gr[h:h + 1, :] - gr[H + h:H + h + 1, :]
        m_prev = m_sc[h][0:1, 0:1]
        c_prev = c_sc[h]
        n_prev = n_sc[h]

        dlog = jnp.where(causal, b_col + a_row, -jnp.inf)
        inter = b_col + m_prev
        m_comb = jnp.maximum(inter, jnp.max(dlog, axis=1, keepdims=True))
        sm = _dot_nt(qh, kh) * jnp.exp(dlog - m_comb)
        w_inter = jnp.exp(inter - m_comb)
        num = _dot(sm.astype(BF16), vh) + w_inter * _dot_nt(qh, c_prev.astype(BF16))
        den = jnp.sum(sm, axis=1, keepdims=True) + w_inter * jnp.sum(qf * n_prev, axis=1, keepdims=True)
        hid = num / jnp.maximum(jnp.abs(den), jnp.exp(-m_comb))
        o_ref[:, sl] = (jax.nn.sigmoid(og_ref[:, sl]) * hid).astype(BF16)

        bl = b_col[L - 1:L, :]
        g = bl - b_col + i_col
        m_new = jnp.maximum(bl + m_prev, jnp.max(g, axis=0, keepdims=True))
        wk = jnp.exp(g - m_new)
        decay = jnp.exp(bl + m_prev - m_new)
        c_sc[h] = decay * c_prev + _dot_tn((wk * vv[:, sl]).astype(BF16), kh)
        n_sc[h] = decay * n_prev + jnp.sum(wk * kc[:, sl], axis=0, keepdims=True)
        m_sc[h] = jnp.broadcast_to(m_new, m_sc.shape[1:])


def _mlstm(z, conv_w, gates_col, gates_row, *, L):
    S = z.shape[0]
    r = L // CONV_HALO
    W = MLSTM_HEADS * MLSTM_HEAD_DIM
    halo = lambda c: jnp.maximum(c * r - 1, 0)
    return pl.pallas_call(
        _mlstm_kernel,
        grid=(S // L,),
        in_specs=[
            pl.BlockSpec((L, W), lambda c: (c, COL_ML_Q // W)),
            pl.BlockSpec((L, W), lambda c: (c, COL_ML_K // W)),
            pl.BlockSpec((L, W), lambda c: (c, COL_ML_V // W)),
            pl.BlockSpec((L, W), lambda c: (c, COL_ML_O // W)),
            pl.BlockSpec((CONV_HALO, W), lambda c: (halo(c), COL_ML_Q // W)),
            pl.BlockSpec((CONV_HALO, W), lambda c: (halo(c), COL_ML_K // W)),
            pl.BlockSpec((MLSTM_CONV, 2 * W), lambda c: (0, 0)),
            pl.BlockSpec((L, N_GATES), lambda c: (c, 0)),
            pl.BlockSpec((N_GATES, L), lambda c: (0, c)),
        ],
        out_specs=pl.BlockSpec((L, W), lambda c: (c, 0)),
        out_shape=jax.ShapeDtypeStruct((S, W), BF16),
        scratch_shapes=[pltpu.VMEM((MLSTM_HEADS, MLSTM_HEAD_DIM, MLSTM_HEAD_DIM), F32),
                        pltpu.VMEM((MLSTM_HEADS, 1, MLSTM_HEAD_DIM), F32),
                        pltpu.VMEM((MLSTM_HEADS, 8, LANES), F32)],
        compiler_params=_cparams(("arbitrary",)),
        name="mlstm",
    )(z, z, z, z, z, z, conv_w, gates_col, gates_row)


def _fox_kernel(q_ref, k_ref, v_ref, cq_ref, ck_ref, o_ref, m_sc, acc_sc, s_sc, *, t):
    i = pl.program_id(1)
    nch = t // LANES
    m_sc[...] = jnp.full_like(m_sc, -jnp.inf)
    acc_sc[...] = jnp.zeros_like(acc_sc)

    q = q_ref[...] * (LOG2E * HEAD_DIM ** -0.5)
    lane = lax.broadcasted_iota(jnp.int32, (t, LANES), 1)
    row = lax.broadcasted_iota(jnp.int32, (t, LANES), 0)
    low = lane < HEAD_DIM
    own = [low, ~low]
    qa = [jnp.where(own[a], q, 0.0).astype(BF16) for a in range(2)]
    cq = cq_ref[...] * LOG2E
    ca = [jnp.broadcast_to(cq[:, a:a + 1], (t, LANES)) for a in range(2)]

    def scores(j, slot):
        j0 = pl.multiple_of(j * t, t)
        kb = k_ref[pl.ds(j0, t), :]
        for a in range(2):
            s_sc[slot, a] = _dot_nt(qa[a], kb)

    def softmax_pv(j, slot, masked):
        j0 = pl.multiple_of(j * t, t)
        vb = v_ref[pl.ds(j0, t), :]
        ck = ck_ref[:, pl.ds(j0, t)] * LOG2E
        ps, alphas = [], []
        for a in range(2):
            sc = []
            for c in range(nch):
                x = s_sc[slot, a, :, c * LANES:(c + 1) * LANES] - ck[a:a + 1, c * LANES:(c + 1) * LANES]
                if masked:
                    x = jnp.where(lane + c * LANES <= row, x, -jnp.inf)
                sc.append(x)
            mx = sc[0]
            for c in range(1, nch):
                mx = jnp.maximum(mx, sc[c])
            m_old = m_sc[a]
            m_new = jnp.maximum(m_old, jnp.max(mx, axis=1, keepdims=True) + ca[a])
            alpha = jnp.exp2(m_old - m_new)
            mm = m_new - ca[a]
            ps += [jnp.exp2(x - mm).astype(BF16) for x in sc]
            alphas.append(alpha)
            m_sc[a] = m_new
        one, zero = jnp.ones_like(vb), jnp.zeros_like(vb)
        vblk = jnp.concatenate([jnp.concatenate([jnp.where(low, vb, one), zero], axis=1),
                                jnp.concatenate([zero, jnp.where(low, one, vb)], axis=1)], axis=0)
        acc_sc[...] = jnp.concatenate(alphas, axis=1) * acc_sc[...] + _dot(jnp.concatenate(ps, axis=1), vblk)

    scores(0, 0)

    def pair(j):
        scores(j + 1, 1)
        softmax_pv(j, 0, False)
        scores(j + 2, 0)
        softmax_pv(j + 1, 1, False)

    def body(j4, carry):
        pair(4 * j4)
        pair(4 * j4 + 2)
        return carry

    lax.fori_loop(0, i // 4, body, 0)

    @pl.when(i % 4 >= 2)
    def _():
        pair(4 * (i // 4))

    @pl.when(i % 2 == 1)
    def _():
        scores(i, 1)
        softmax_pv(i - 1, 0, False)
        softmax_pv(i, 1, True)

    @pl.when(i % 2 == 0)
    def _():
        softmax_pv(i, 0, True)

    outs = []
    for a in range(2):
        acc = acc_sc[:, a * LANES:(a + 1) * LANES]
        outs.append(acc / pltpu.roll(acc, HEAD_DIM, axis=1))
    o_ref[...] = jnp.where(low, outs[0], outs[1]).astype(BF16)


def _fox(z, zb, c_col, c_row, *, t):
    S = z.shape[0]
    npair = FOX_HEADS // 2
    cq0, ck0, cv0 = COL_FOX_Q // LANES, COL_FOX_K // LANES, COL_FOX_V // LANES
    return pl.pallas_call(
        functools.partial(_fox_kernel, t=t),
        grid=(npair, S // t),
        in_specs=[
            pl.BlockSpec((t, LANES), lambda p, i: (i, cq0 + p)),
            pl.BlockSpec((S, LANES), lambda p, i: (0, ck0 + p)),
            pl.BlockSpec((S, LANES), lambda p, i: (0, cv0 + p)),
            pl.BlockSpec((None, t, 2), lambda p, i: (p, i, 0)),
            pl.BlockSpec((None, 2, S), lambda p, i: (p, 0, 0)),
        ],
        out_specs=pl.BlockSpec((t, LANES), lambda p, i: (i, p)),
        out_shape=jax.ShapeDtypeStruct((S, MIX_WIDTH), BF16),
        scratch_shapes=[pltpu.VMEM((2, t, LANES), F32), pltpu.VMEM((t, 2 * LANES), F32),
                        pltpu.VMEM((2, 2, t, t), F32)],
        compiler_params=_cparams(("parallel", "arbitrary")),
        name="fox",
    )(z, zb, zb, c_col, c_row)


def _merge_kernel(hb_ref, ya_ref, yb_ref, yc_ref, yd_ref, wg_ref, wb_ref, o_ref):
    hb = hb_ref[...]
    acc = None
    for b, y_ref in enumerate((ya_ref, yb_ref, yc_ref, yd_ref)):
        gate = jax.nn.sigmoid(_dot(hb, wg_ref[b]))
        term = gate * _dot(y_ref[...], wb_ref[b])
        acc = term if acc is None else acc + term
    o_ref[...] = acc.astype(BF16)


def _merge(hb, ys, wg, wb, layer, *, tm, tn):
    S, D = hb.shape
    _, nb, W, _ = wb.shape
    yspec = pl.BlockSpec((tm, W), lambda n, i: (i, 0))
    return pl.pallas_call(
        _merge_kernel,
        grid=(D // tn, S // tm),
        in_specs=[pl.BlockSpec((tm, D), lambda n, i: (i, 0)), yspec, yspec, yspec, yspec,
                  pl.BlockSpec((None, nb, D, tn), lambda n, i: (layer, 0, 0, n)),
                  pl.BlockSpec((None, nb, W, tn), lambda n, i: (layer, 0, 0, n))],
        out_specs=pl.BlockSpec((tm, tn), lambda n, i: (i, n)),
        out_shape=jax.ShapeDtypeStruct((S, D), BF16),
        compiler_params=_cparams(("parallel", "arbitrary")),
        name="merge",
    )(hb, *ys, wg, wb)


def _outproj_kernel(m_ref, h_ref, w_ref, g_ref, b_ref, o_ref, *, alpha):
    g, b = g_ref[...], b_ref[...]
    rows = 128
    for r in range(m_ref.shape[0] // rows):
        rs = slice(r * rows, (r + 1) * rows)
        y = alpha * h_ref[rs, :] + _dot(m_ref[rs, :], w_ref[...])
        o_ref[rs, :] = _layer_norm(y, g, b)


def _outproj(merged, h, w, layer, g, b, *, alpha, tm):
    S, D = h.shape
    return pl.pallas_call(
        functools.partial(_outproj_kernel, alpha=alpha),
        grid=(S // tm,),
        in_specs=[pl.BlockSpec((tm, D), lambda i: (i, 0)), pl.BlockSpec((tm, D), lambda i: (i, 0)),
                  pl.BlockSpec((None, D, D), lambda i: (layer, 0, 0)),
                  pl.BlockSpec((1, D), lambda i: (0, 0)), pl.BlockSpec((1, D), lambda i: (0, 0))],
        out_specs=pl.BlockSpec((tm, D), lambda i: (i, 0)),
        out_shape=jax.ShapeDtypeStruct((S, D), F32),
        compiler_params=_cparams(("parallel",)),
        name="outproj",
    )(merged, h, w, g.reshape(1, D), b.reshape(1, D))


def _swa_head_order():
    g = SWA_HEADS // SWA_KV_HEADS
    order = []
    for s in range(g):
        order += [s, g + s]
    return order


def _swa_col_perm():
    return np.concatenate([np.arange(h * HEAD_DIM, (h + 1) * HEAD_DIM) for h in _swa_head_order()])


def _rope_tables(S):
    half = HEAD_DIM // 2
    inv = ROPE_THETA ** (-jnp.arange(half, dtype=F32) / half)
    ang = jnp.arange(S, dtype=jnp.int32).astype(F32)[:, None] * inv[None, :]
    cos, sin = jnp.cos(ang), jnp.sin(ang)
    reps = LANES // HEAD_DIM
    cos_t = jnp.tile(jnp.concatenate([cos, cos], axis=1), (1, reps))
    sin_t = jnp.tile(jnp.concatenate([-sin, sin], axis=1), (1, reps))
    return cos_t, sin_t


def _pick(n, prefs):
    for p in prefs:
        if n % p == 0:
            return p
    return n


def kernel(x, ln_g, ln_b, ffn1_w_in, ffn1_w_out, w_in, pool_w, pool_scale, swa_sinks, mlstm_conv, mlstm_i_bias, mlstm_f_bias, fox_f_bias, w_branch, w_gate, w_out, ffn2_w_in, ffn2_w_out):
    B, S, D = x.shape
    depth = ln_g.shape[0]
    assert B == 1 and w_in.shape[2] == 4880 and S % ML_CHUNK == 0
    alpha = float((2 * depth) ** 0.25)
    cos_t, sin_t = _rope_tables(S)
    tm = _pick(S, (512, 256))
    tm_ffn = _pick(S, (1024, 512, 256))
    t_mix = _pick(S, (512, 256))
    tq = _pick(S, (512, 256))

    g = SWA_HEADS // SWA_KV_HEADS
    wi = w_in.astype(BF16)
    swa_q = wi[..., 512:1024].reshape(depth, D, SWA_KV_HEADS, g, HEAD_DIM).swapaxes(2, 3).reshape(depth, D, MIX_WIDTH)
    wz = jnp.concatenate([
        wi[..., 0:512], swa_q, wi[..., 1280:3328], wi[..., 3336:4872],
        wi[..., 1024:1280], wi[..., 3328:3336], wi[..., 4872:4880],
        jnp.zeros((depth, D, Z_COLS - COL_GATES - N_GATES), BF16)], axis=2)
    wbr = w_branch.astype(BF16)
    swa_rows = wbr[:, 1].reshape(depth, SWA_KV_HEADS, g, HEAD_DIM, D).swapaxes(1, 2).reshape(depth, 1, MIX_WIDTH, D)
    wb = jnp.concatenate([wbr[:, :1], swa_rows, wbr[:, 2:]], axis=1)
    wg = w_gate.astype(BF16)
    wo = w_out.astype(BF16)
    f1_in, f1_out = ffn1_w_in.astype(BF16), ffn1_w_out.astype(BF16)
    f2_in, f2_out = ffn2_w_in.astype(BF16), ffn2_w_out.astype(BF16)

    h = x.reshape(S, D)
    for l in range(depth):
        gate_bias = jnp.concatenate([mlstm_i_bias[l], mlstm_f_bias[l], fox_f_bias[l]]).reshape(1, N_GATES)
        sinks = swa_sinks[l]

        h, hb = _ffn(h, f1_in, f1_out, l, ln_g[l, 0], ln_b[l, 0], alpha=alpha, tm=tm_ffn, tf=512)
        z, zb = _inproj(hb, wz, l, tm=tm, tn=Z_COLS // 2)
        gates_col = _gates(z, gate_bias, T=ML_CHUNK)
        gates_row = gates_col.T
        nh = FOX_HEADS // 2
        c_col = gates_col[:, 2 * MLSTM_HEADS:].reshape(S, nh, 2).transpose(1, 0, 2)

        ya = _pool(z, pool_w[l].astype(BF16), pool_scale[l], T=t_mix)
        yb = _swa(z, sinks, cos_t, sin_t, T=t_mix)
        yc = _mlstm(z, mlstm_conv[l], gates_col, gates_row, L=ML_CHUNK)
        c_row = gates_row[2 * MLSTM_HEADS:].reshape(nh, 2, S)
        yd = _fox(z, zb, c_col, c_row, t=tq)
        merged = _merge(hb, (ya, yb, yc, yd), wg, wb, l, tm=_pick(S, (1024, 512, 256)), tn=512)
        h = _outproj(merged, h, wo, l, ln_g[l, 1], ln_b[l, 1], alpha=alpha, tm=tm)
        h, hb = _ffn(h, f2_in, f2_out, l, ln_g[l, 2], ln_b[l, 2], alpha=alpha, tm=tm_ffn, tf=512)
    return h.reshape(B, S, D)
```

```python
import functools

import jax
import jax.numpy as jnp
import numpy as np
from jax import lax
from jax.experimental import pallas as pl
from jax.experimental.pallas import tpu as pltpu

F32 = jnp.float32
BF16 = jnp.bfloat16

MIX_WIDTH = 512
HEAD_DIM = 64
POOL_WINDOWS = (2, 4, 8, 16)
POOL_CH = 128
SWA_HEADS = 8
SWA_KV_HEADS = 2
SWA_BLOCK = 128
ROPE_THETA = 10000.0
MLSTM_HEADS = 4
MLSTM_HEAD_DIM = 128
MLSTM_CONV = 4
FOX_HEADS = 8
LN_EPS = 1e-5
FFN_HALF = 0.5
LOG2E = 1.4426950408889634

LANES = 128
VMEM_LIMIT = 60 * 1024 * 1024

COL_POOL = 0
COL_SWA_Q = 512
COL_ML_Q = 1024
COL_ML_K = 1536
COL_ML_V = 2048
COL_ML_O = 2560
COL_FOX_Q = 3072
COL_FOX_K = 3584
COL_FOX_V = 4096
COL_SWA_K = 4608
COL_SWA_V = 4736
COL_GATES = 4864
Z_COLS = 5120
N_GATES = 16

ML_CHUNK = 256


def _cparams(sem):
    return pltpu.CompilerParams(dimension_semantics=sem, vmem_limit_bytes=VMEM_LIMIT)


def _layer_norm(y, g, b):
    mu = jnp.mean(y, axis=-1, keepdims=True)
    d = y - mu
    var = jnp.mean(d * d, axis=-1, keepdims=True)
    return d * lax.rsqrt(var + LN_EPS) * g + b


def _dot(a, b):
    return jnp.dot(a, b, preferred_element_type=F32)


def _dot_nt(a, b):
    return lax.dot_general(a, b, (((1,), (1,)), ((), ())), preferred_element_type=F32)


def _dot_tn(a, b):
    return lax.dot_general(a, b, (((0,), (0,)), ((), ())), preferred_element_type=F32)


def _ffn_kernel(x_ref, wg_ref, wu_ref, wo_ref, g_ref, b_ref, o_ref, ob_ref, acc_ref, *, alpha, nf):
    f = pl.program_id(1)
    tm = x_ref.shape[0]
    half = tm // 2
    rows = 64

    def layer_norm_rows(r0, n):
        g, b = g_ref[...], b_ref[...]
        for c in range(n // rows):
            rs = slice(r0 + c * rows, r0 + (c + 1) * rows)
            y = alpha * x_ref[rs, :] + FFN_HALF * acc_ref[rs, :]
            o = _layer_norm(y, g, b)
            o_ref[rs, :] = o
            ob_ref[rs, :] = o.astype(BF16)

    def step(first, last):
        for r in range(2):
            rs = slice(r * half, (r + 1) * half)
            xb = x_ref[rs, :].astype(BF16)
            gate = _dot(xb, wg_ref[...])
            up = _dot(xb, wu_ref[...])
            act = (gate * jax.nn.sigmoid(gate) * up).astype(BF16)
            nw = acc_ref.shape[1] // 2
            for n in range(2):
                cols = slice(n * nw, (n + 1) * nw)
                d = _dot(act, wo_ref[:, cols])
                if first:
                    acc_ref[rs, cols] = d
                else:
                    acc_ref[rs, cols] += d
            if last:
                layer_norm_rows(r * half, half)

    if nf == 1:
        step(True, True)
    else:
        pl.when(f == 0)(lambda: step(True, False))
        pl.when((f > 0) & (f < nf - 1))(lambda: step(False, False))
        pl.when(f == nf - 1)(lambda: step(False, True))


def _ffn(x, w_in, w_out, layer, g, b, *, alpha, tm, tf):
    S, D = x.shape
    DF = w_out.shape[1]
    nf = DF // tf
    assert S % tm == 0 and DF % tf == 0
    return pl.pallas_call(
        functools.partial(_ffn_kernel, alpha=alpha, nf=nf),
        grid=(S // tm, nf),
        in_specs=[
            pl.BlockSpec((tm, D), lambda i, f: (i, 0)),
            pl.BlockSpec((None, D, tf), lambda i, f: (layer, 0, f)),
            pl.BlockSpec((None, D, tf), lambda i, f: (layer, 0, f + nf)),
            pl.BlockSpec((None, tf, D), lambda i, f: (layer, f, 0)),
            pl.BlockSpec((1, D), lambda i, f: (0, 0)),
            pl.BlockSpec((1, D), lambda i, f: (0, 0)),
        ],
        out_specs=[pl.BlockSpec((tm, D), lambda i, f: (i, 0), pipeline_mode=pl.Buffered(1)),
                   pl.BlockSpec((tm, D), lambda i, f: (i, 0), pipeline_mode=pl.Buffered(1))],
        out_shape=[jax.ShapeDtypeStruct((S, D), F32), jax.ShapeDtypeStruct((S, D), BF16)],
        scratch_shapes=[pltpu.VMEM((tm, D), F32)],
        compiler_params=_cparams(("parallel", "arbitrary")),
        name="ffn",
    )(x, w_in, w_in, w_out, g.reshape(1, D), b.reshape(1, D))


def _inproj_kernel(x_ref, w_ref, o_ref, ob_ref):
    z = _dot(x_ref[...], w_ref[...])
    o_ref[...] = z
    ob_ref[...] = z.astype(BF16)


def _inproj(xb, w, layer, *, tm, tn):
    S, D = xb.shape
    N = w.shape[2]
    assert S % tm == 0 and N % tn == 0
    ospec = pl.BlockSpec((tm, tn), lambda j, i: (i, j))
    return pl.pallas_call(
        _inproj_kernel,
        grid=(N // tn, S // tm),
        in_specs=[pl.BlockSpec((tm, D), lambda j, i: (i, 0)),
                  pl.BlockSpec((None, D, tn), lambda j, i: (layer, 0, j))],
        out_specs=[ospec, ospec],
        out_shape=[jax.ShapeDtypeStruct((S, N), F32), jax.ShapeDtypeStruct((S, N), BF16)],
        compiler_params=_cparams(("parallel", "arbitrary")),
        name="inproj",
    )(xb, w)


def _split3(x):
    hi = x.astype(BF16)
    r1 = x - hi.astype(F32)
    mid = r1.astype(BF16)
    lo = (r1 - mid.astype(F32)).astype(BF16)
    return hi, mid, lo


def _gates_kernel(z_ref, bias_ref, o_ref, carry_ref):
    c = pl.program_id(0)
    T = z_ref.shape[0]

    @pl.when(c == 0)
    def _():
        carry_ref[...] = jnp.zeros_like(carry_ref)

    pre = z_ref[:, 0:N_GATES] + bias_ref[...]
    lf = jax.nn.log_sigmoid(pre)
    row = lax.broadcasted_iota(jnp.int32, (T, T), 0)
    col = lax.broadcasted_iota(jnp.int32, (T, T), 1)
    tri = jnp.where(col <= row, 1.0, 0.0).astype(BF16)
    hi, mid, lo = _split3(lf)
    cs = _dot(tri, hi) + _dot(tri, mid) + _dot(tri, lo)
    lane = lax.broadcasted_iota(jnp.int32, (T, N_GATES), 1)
    is_fox = lane >= 2 * MLSTM_HEADS
    cs = cs + jnp.where(is_fox, carry_ref[...], 0.0)
    carry_ref[...] = cs[T - 1:T, :]
    o_ref[...] = jnp.where(lane < MLSTM_HEADS, pre, cs)


def _gates(z, bias, *, T):
    S = z.shape[0]
    return pl.pallas_call(
        _gates_kernel,
        grid=(S // T,),
        in_specs=[pl.BlockSpec((T, LANES), lambda c: (c, COL_GATES // LANES)),
                  pl.BlockSpec((1, N_GATES), lambda c: (0, 0))],
        out_specs=pl.BlockSpec((T, N_GATES), lambda c: (c, 0)),
        out_shape=jax.ShapeDtypeStruct((S, N_GATES), F32),
        scratch_shapes=[pltpu.VMEM((1, N_GATES), F32)],
        compiler_params=_cparams(("arbitrary",)),
        name="gates",
    )(z, bias)


POOL_HALO = 16


def _pool_kernel(u_ref, h_ref, w_ref, s_ref, o_ref):
    i = pl.program_id(0)
    T = u_ref.shape[0]
    u = u_ref[...]
    halo = jnp.where(i > 0, h_ref[...], 0.0)
    full = jnp.concatenate([halo, u], axis=0)
    t = i * T + lax.broadcasted_iota(jnp.int32, (T, 1), 0) + 1
    outs = []
    for g, win in enumerate(POOL_WINDOWS):
        cols = slice(g * POOL_CH, (g + 1) * POOL_CH)
        fg = full[:, cols]
        acc = fg[POOL_HALO:POOL_HALO + T]
        for j in range(1, win):
            acc = acc + fg[POOL_HALO - j:POOL_HALO - j + T]
        count = jnp.minimum(t, win).astype(F32)
        pooled = acc / count - u[:, cols]
        outs.append(_dot(pooled.astype(BF16), w_ref[g]))
    y = jnp.concatenate(outs, axis=1) * s_ref[...]
    o_ref[...] = y.astype(BF16)


def _pool(z, w, scale, *, T):
    S = z.shape[0]
    r = T // POOL_HALO
    return pl.pallas_call(
        _pool_kernel,
        grid=(S // T,),
        in_specs=[
            pl.BlockSpec((T, MIX_WIDTH), lambda i: (i, 0)),
            pl.BlockSpec((POOL_HALO, MIX_WIDTH), lambda i: (jnp.maximum(i * r - 1, 0), 0)),
            pl.BlockSpec((len(POOL_WINDOWS), POOL_CH, POOL_CH), lambda i: (0, 0, 0)),
            pl.BlockSpec((1, MIX_WIDTH), lambda i: (0, 0)),
        ],
        out_specs=pl.BlockSpec((T, MIX_WIDTH), lambda i: (i, 0)),
        out_shape=jax.ShapeDtypeStruct((S, MIX_WIDTH), BF16),
        compiler_params=_cparams(("parallel",)),
        name="pool",
    )(z, z, w, scale.reshape(1, MIX_WIDTH))


def _rope(x, cos, sin_signed):
    n = x.shape[1] // LANES
    outs = []
    for s in range(n):
        xs = x[:, s * LANES:(s + 1) * LANES]
        outs.append(xs * cos + pltpu.roll(xs, LANES // 2, axis=1) * sin_signed)
    return outs[0] if n == 1 else jnp.concatenate(outs, axis=1)


def _swa_kernel(sink_ref, q_ref, k_ref, v_ref, kh_ref, vh_ref, cos_ref, sin_ref, cosh_ref, sinh_ref, o_ref, *, nblk):
    i = pl.program_id(0)
    B = SWA_BLOCK
    G = SWA_HEADS // SWA_KV_HEADS
    cos = cos_ref[...]
    sin = sin_ref[...]
    q = (_rope(q_ref[...], cos, sin) * (HEAD_DIM ** -0.5)).astype(BF16)
    k = jnp.concatenate([_rope(kh_ref[...], cosh_ref[...], sinh_ref[...]), _rope(k_ref[...], cos, sin)],
                        axis=0).astype(BF16)
    v = jnp.concatenate([vh_ref[...], v_ref[...]], axis=0).astype(BF16)

    lane = lax.broadcasted_iota(jnp.int32, (G * B, LANES), 1)
    low = lane < HEAD_DIM
    low_qk = (lane % HEAD_DIM) < (HEAD_DIM // 2)
    low_v = lax.broadcasted_iota(jnp.int32, (2 * B, LANES), 1) < HEAD_DIM
    rowq = lax.broadcasted_iota(jnp.int32, (G * B, 2 * B), 0) % B
    colk = lax.broadcasted_iota(jnp.int32, (G * B, 2 * B), 1)
    band = (colk > rowq) & (colk <= rowq + B)
    grp = lax.broadcasted_iota(jnp.int32, (G * B, LANES), 0) // B
    sinks = []
    for kvh in range(SWA_KV_HEADS):
        col = jnp.zeros((G * B, LANES), F32)
        for g in range(G):
            col = jnp.where(grp == g, sink_ref[kvh * G + g], col)
        sinks.append(col)

    for blk in range(nblk):
        r0 = blk * B
        qb = jnp.concatenate([q[r0:r0 + B, s * LANES:(s + 1) * LANES] for s in range(G)], axis=0)
        kb = k[r0:r0 + 2 * B]
        vb = v[r0:r0 + 2 * B]
        valid = band
        if blk == 0:
            valid = band & ((colk >= B) | (i > 0))
        pvs = []
        for kvh in range(SWA_KV_HEADS):
            qm = jnp.where(low_qk if kvh == 0 else ~low_qk, qb, jnp.zeros_like(qb))
            s = _dot_nt(qm, kb)
            s = jnp.where(valid, s, -jnp.inf)
            sc = [s[:, c * LANES:(c + 1) * LANES] for c in range(2 * B // LANES)]
            mx = sc[0]
            for x in sc[1:]:
                mx = jnp.maximum(mx, x)
            m = jnp.maximum(jnp.max(mx, axis=1, keepdims=True), sinks[kvh])
            p = jnp.concatenate([jnp.exp(x - m).astype(BF16) for x in sc], axis=1)
            pv = _dot(p, jnp.where(low_v if kvh == 0 else ~low_v, vb, jnp.ones_like(vb)))
            den = pltpu.roll(pv, HEAD_DIM, axis=1) + jnp.exp(sinks[kvh] - m)
            pvs.append(pv / den)
        o = jnp.where(low, pvs[0], pvs[1])
        for s_ in range(G):
            o_ref[r0:r0 + B, s_ * LANES:(s_ + 1) * LANES] = o[s_ * B:(s_ + 1) * B].astype(BF16)


def _swa(z, sinks, cos, sin, *, T):
    S = z.shape[0]
    nblk = T // SWA_BLOCK
    cq, ck, cv = COL_SWA_Q // MIX_WIDTH, COL_SWA_K // LANES, COL_SWA_V // LANES
    halo = lambda i: jnp.maximum(i * nblk - 1, 0)
    return pl.pallas_call(
        functools.partial(_swa_kernel, nblk=nblk),
        grid=(S // T,),
        in_specs=[
            pl.BlockSpec(memory_space=pltpu.SMEM),
            pl.BlockSpec((T, MIX_WIDTH), lambda i: (i, cq)),
            pl.BlockSpec((T, LANES), lambda i: (i, ck)),
            pl.BlockSpec((T, LANES), lambda i: (i, cv)),
            pl.BlockSpec((SWA_BLOCK, LANES), lambda i: (halo(i), ck)),
            pl.BlockSpec((SWA_BLOCK, LANES), lambda i: (halo(i), cv)),
            pl.BlockSpec((T, LANES), lambda i: (i, 0)),
            pl.BlockSpec((T, LANES), lambda i: (i, 0)),
            pl.BlockSpec((SWA_BLOCK, LANES), lambda i: (halo(i), 0)),
            pl.BlockSpec((SWA_BLOCK, LANES), lambda i: (halo(i), 0)),
        ],
        out_specs=pl.BlockSpec((T, MIX_WIDTH), lambda i: (i, 0)),
        out_shape=jax.ShapeDtypeStruct((S, MIX_WIDTH), BF16),
        compiler_params=_cparams(("parallel",)),
        name="swa",
    )(sinks, z, z, z, z, z, cos, sin, cos, sin)


CONV_HALO = 8


def _mlstm_kernel(q_ref, k_ref, v_ref, og_ref, qh_ref, kh_ref, cw_ref, gc_ref, gr_ref, o_ref, c_sc, n_sc, m_sc):
    c = pl.program_id(0)
    L = q_ref.shape[0]
    H, dh = MLSTM_HEADS, MLSTM_HEAD_DIM

    @pl.when(c == 0)
    def _():
        c_sc[...] = jnp.zeros_like(c_sc)
        n_sc[...] = jnp.zeros_like(n_sc)
        m_sc[...] = jnp.zeros_like(m_sc)

    def conv_silu(x_ref, halo_ref, w):
        halo = jnp.where(c > 0, halo_ref[...], 0.0)
        full = jnp.concatenate([halo, x_ref[...]], axis=0)
        acc = jnp.zeros((L, x_ref.shape[1]), F32)
        for j in range(MLSTM_CONV):
            off = CONV_HALO - (MLSTM_CONV - 1) + j
            acc = acc + w[j:j + 1, :] * full[off:off + L]
        return acc * jax.nn.sigmoid(acc)

    cw = cw_ref[...]
    qc = conv_silu(q_ref, qh_ref, cw[:, :H * dh]) * (dh ** -0.5)
    kc = conv_silu(k_ref, kh_ref, cw[:, H * dh:])
    vv = v_ref[...]
    gc = gc_ref[...]
    gr = gr_ref[...]
    row = lax.broadcasted_iota(jnp.int32, (L, L), 0)
    col = lax.broadcasted_iota(jnp.int32, (L, L), 1)
    causal = col <= row

    for h in range(H):
        sl = slice(h * dh, (h + 1) * dh)
        qf = qc[:, sl]
        qh, kh, vh = qf.astype(BF16), kc[:, sl].astype(BF16), vv[:, sl].astype(BF16)
        i_col = gc[:, h:h + 1]
        b_col = gc[:, H + h:H + h + 1]
        a_row = gr[h:h + 1, :] - gr[H + h:H + h + 1, :]
        m_prev = m_sc[h][0:1, 0:1]
        c_prev = c_sc[h]
        n_prev = n_sc[h]

        dlog = jnp.where(causal, b_col + a_row, -jnp.inf)
        inter = b_col + m_prev
        m_comb = jnp.maximum(inter, jnp.max(dlog, axis=1, keepdims=True))
        sm = _dot_nt(qh, kh) * jnp.exp(dlog - m_comb)
        w_inter = jnp.exp(inter - m_comb)
        num = _dot(sm.astype(BF16), vh) + w_inter * _dot_nt(qh, c_prev.astype(BF16))
        den = jnp.sum(sm, axis=1, keepdims=True) + w_inter * jnp.sum(qf * n_prev, axis=1, keepdims=True)
        hid = num / jnp.maximum(jnp.abs(den), jnp.exp(-m_comb))
        o_ref[:, sl] = (jax.nn.sigmoid(og_ref[:, sl]) * hid).astype(BF16)

        bl = b_col[L - 1:L, :]
        g = bl - b_col + i_col
        m_new = jnp.maximum(bl + m_prev, jnp.max(g, axis=0, keepdims=True))
        wk = jnp.exp(g - m_new)
        decay = jnp.exp(bl + m_prev - m_new)
        c_sc[h] = decay * c_prev + _dot_tn((wk * vv[:, sl]).astype(BF16), kh)
        n_sc[h] = decay * n_prev + jnp.sum(wk * kc[:, sl], axis=0, keepdims=True)
        m_sc[h] = jnp.broadcast_to(m_new, m_sc.shape[1:])


def _mlstm(z, conv_w, gates_col, gates_row, *, L):
    S = z.shape[0]
    r = L // CONV_HALO
    W = MLSTM_HEADS * MLSTM_HEAD_DIM
    halo = lambda c: jnp.maximum(c * r - 1, 0)
    return pl.pallas_call(
        _mlstm_kernel,
        grid=(S // L,),
        in_specs=[
            pl.BlockSpec((L, W), lambda c: (c, COL_ML_Q // W)),
            pl.BlockSpec((L, W), lambda c: (c, COL_ML_K // W)),
            pl.BlockSpec((L, W), lambda c: (c, COL_ML_V // W)),
            pl.BlockSpec((L, W), lambda c: (c, COL_ML_O // W)),
            pl.BlockSpec((CONV_HALO, W), lambda c: (halo(c), COL_ML_Q // W)),
            pl.BlockSpec((CONV_HALO, W), lambda c: (halo(c), COL_ML_K // W)),
            pl.BlockSpec((MLSTM_CONV, 2 * W), lambda c: (0, 0)),
            pl.BlockSpec((L, N_GATES), lambda c: (c, 0)),
            pl.BlockSpec((N_GATES, L), lambda c: (0, c)),
        ],
        out_specs=pl.BlockSpec((L, W), lambda c: (c, 0)),
        out_shape=jax.ShapeDtypeStruct((S, W), BF16),
        scratch_shapes=[pltpu.VMEM((MLSTM_HEADS, MLSTM_HEAD_DIM, MLSTM_HEAD_DIM), F32),
                        pltpu.VMEM((MLSTM_HEADS, 1, MLSTM_HEAD_DIM), F32),
                        pltpu.VMEM((MLSTM_HEADS, 8, LANES), F32)],
        compiler_params=_cparams(("arbitrary",)),
        name="mlstm",
    )(z, z, z, z, z, z, conv_w, gates_col, gates_row)


def _fox_kernel(q_ref, k_ref, v_ref, cq_ref, ck_ref, o_ref, m_sc, acc_sc, s_sc, *, t):
    i = pl.program_id(1)
    nch = t // LANES
    m_sc[...] = jnp.full_like(m_sc, -jnp.inf)
    acc_sc[...] = jnp.zeros_like(acc_sc)

    q = q_ref[...] * (LOG2E * HEAD_DIM ** -0.5)
    lane = lax.broadcasted_iota(jnp.int32, (t, LANES), 1)
    row = lax.broadcasted_iota(jnp.int32, (t, LANES), 0)
    low = lane < HEAD_DIM
    own = [low, ~low]
    qa = [jnp.where(own[a], q, 0.0).astype(BF16) for a in range(2)]
    cq = cq_ref[...] * LOG2E
    ca = [jnp.broadcast_to(cq[:, a:a + 1], (t, LANES)) for a in range(2)]

    def scores(j, slot):
        j0 = pl.multiple_of(j * t, t)
        kb = k_ref[pl.ds(j0, t), :]
        for a in range(2):
            s_sc[slot, a] = _dot_nt(qa[a], kb)

    def softmax_pv(j, slot, masked):
        j0 = pl.multiple_of(j * t, t)
        vb = v_ref[pl.ds(j0, t), :]
        ck = ck_ref[:, pl.ds(j0, t)] * LOG2E
        ps, alphas = [], []
        for a in range(2):
            sc = []
            for c in range(nch):
                x = s_sc[slot, a, :, c * LANES:(c + 1) * LANES] - ck[a:a + 1, c * LANES:(c + 1) * LANES]
                if masked:
                    x = jnp.where(lane + c * LANES <= row, x, -jnp.inf)
                sc.append(x)
            mx = sc[0]
            for c in range(1, nch):
                mx = jnp.maximum(mx, sc[c])
            m_old = m_sc[a]
            m_new = jnp.maximum(m_old, jnp.max(mx, axis=1, keepdims=True) + ca[a])
            alpha = jnp.exp2(m_old - m_new)
            mm = m_new - ca[a]
            ps += [jnp.exp2(x - mm).astype(BF16) for x in sc]
            alphas.append(alpha)
            m_sc[a] = m_new
        one, zero = jnp.ones_like(vb), jnp.zeros_like(vb)
        vblk = jnp.concatenate([jnp.concatenate([jnp.where(low, vb, one), zero], axis=1),
                                jnp.concatenate([zero, jnp.where(low, one, vb)], axis=1)], axis=0)
        acc_sc[...] = jnp.concatenate(alphas, axis=1) * acc_sc[...] + _dot(jnp.concatenate(ps, axis=1), vblk)

    scores(0, 0)

    def pair(j):
        scores(j + 1, 1)
        softmax_pv(j, 0, False)
        scores(j + 2, 0)
        softmax_pv(j + 1, 1, False)

    def body(j4, carry):
        pair(4 * j4)
        pair(4 * j4 + 2)
        return carry

    lax.fori_loop(0, i // 4, body, 0)

    @pl.when(i % 4 >= 2)
    def _():
        pair(4 * (i // 4))

    @pl.when(i % 2 == 1)
    def _():
        scores(i, 1)
        softmax_pv(i - 1, 0, False)
        softmax_pv(i, 1, True)

    @pl.when(i % 2 == 0)
    def _():
        softmax_pv(i, 0, True)

    outs = []
    for a in range(2):
        acc = acc_sc[:, a * LANES:(a + 1) * LANES]
        outs.append(acc / pltpu.roll(acc, HEAD_DIM, axis=1))
    o_ref[...] = jnp.where(low, outs[0], outs[1]).astype(BF16)


def _fox(z, zb, c_col, c_row, *, t):
    S = z.shape[0]
    npair = FOX_HEADS // 2
    cq0, ck0, cv0 = COL_FOX_Q // LANES, COL_FOX_K // LANES, COL_FOX_V // LANES
    return pl.pallas_call(
        functools.partial(_fox_kernel, t=t),
        grid=(npair, S // t),
        in_specs=[
            pl.BlockSpec((t, LANES), lambda p, i: (i, cq0 + p)),
            pl.BlockSpec((S, LANES), lambda p, i: (0, ck0 + p)),
            pl.BlockSpec((S, LANES), lambda p, i: (0, cv0 + p)),
            pl.BlockSpec((None, t, 2), lambda p, i: (p, i, 0)),
            pl.BlockSpec((None, 2, S), lambda p, i: (p, 0, 0)),
        ],
        out_specs=pl.BlockSpec((t, LANES), lambda p, i: (i, p)),
        out_shape=jax.ShapeDtypeStruct((S, MIX_WIDTH), BF16),
        scratch_shapes=[pltpu.VMEM((2, t, LANES), F32), pltpu.VMEM((t, 2 * LANES), F32),
                        pltpu.VMEM((2, 2, t, t), F32)],
        compiler_params=_cparams(("parallel", "arbitrary")),
        name="fox",
    )(z, zb, zb, c_col, c_row)


def _merge_kernel(hb_ref, ya_ref, yb_ref, yc_ref, yd_ref, wg_ref, wb_ref, o_ref):
    hb = hb_ref[...]
    acc = None
    for b, y_ref in enumerate((ya_ref, yb_ref, yc_ref, yd_ref)):
        gate = jax.nn.sigmoid(_dot(hb, wg_ref[b]))
        term = gate * _dot(y_ref[...], wb_ref[b])
        acc = term if acc is None else acc + term
    o_ref[...] = acc.astype(BF16)


def _merge(hb, ys, wg, wb, layer, *, tm, tn):
    S, D = hb.shape
    _, nb, W, _ = wb.shape
    yspec = pl.BlockSpec((tm, W), lambda n, i: (i, 0))
    return pl.pallas_call(
        _merge_kernel,
        grid=(D // tn, S // tm),
        in_specs=[pl.BlockSpec((tm, D), lambda n, i: (i, 0)), yspec, yspec, yspec, yspec,
                  pl.BlockSpec((None, nb, D, tn), lambda n, i: (layer, 0, 0, n)),
                  pl.BlockSpec((None, nb, W, tn), lambda n, i: (layer, 0, 0, n))],
        out_specs=pl.BlockSpec((tm, tn), lambda n, i: (i, n)),
        out_shape=jax.ShapeDtypeStruct((S, D), BF16),
        compiler_params=_cparams(("parallel", "arbitrary")),
        name="merge",
    )(hb, *ys, wg, wb)


def _outproj_kernel(m_ref, h_ref, w_ref, g_ref, b_ref, o_ref, *, alpha):
    g, b = g_ref[...], b_ref[...]
    rows = 128
    for r in range(m_ref.shape[0] // rows):
        rs = slice(r * rows, (r + 1) * rows)
        y = alpha * h_ref[rs, :] + _dot(m_ref[rs, :], w_ref[...])
        o_ref[rs, :] = _layer_norm(y, g, b)


def _outproj(merged, h, w, layer, g, b, *, alpha, tm):
    S, D = h.shape
    return pl.pallas_call(
        functools.partial(_outproj_kernel, alpha=alpha),
        grid=(S // tm,),
        in_specs=[pl.BlockSpec((tm, D), lambda i: (i, 0)), pl.BlockSpec((tm, D), lambda i: (i, 0)),
                  pl.BlockSpec((None, D, D), lambda i: (layer, 0, 0)),
                  pl.BlockSpec((1, D), lambda i: (0, 0)), pl.BlockSpec((1, D), lambda i: (0, 0))],
        out_specs=pl.BlockSpec((tm, D), lambda i: (i, 0)),
        out_shape=jax.ShapeDtypeStruct((S, D), F32),
        compiler_params=_cparams(("parallel",)),
        name="outproj",
    )(merged, h, w, g.reshape(1, D), b.reshape(1, D))


def _swa_head_order():
    g = SWA_HEADS // SWA_KV_HEADS
    order = []
    for s in range(g):
        order += [s, g + s]
    return order


def _swa_col_perm():
    return np.concatenate([np.arange(h * HEAD_DIM, (h + 1) * HEAD_DIM) for h in _swa_head_order()])


def _rope_tables(S):
    half = HEAD_DIM // 2
    inv = ROPE_THETA ** (-jnp.arange(half, dtype=F32) / half)
    ang = jnp.arange(S, dtype=jnp.int32).astype(F32)[:, None] * inv[None, :]
    cos, sin = jnp.cos(ang), jnp.sin(ang)
    cos_t = jnp.concatenate([cos, cos, cos, cos], axis=1)
    sin_t = jnp.concatenate([-sin, -sin, sin, sin], axis=1)
    return cos_t, sin_t


def _pick(n, prefs):
    for p in prefs:
        if n % p == 0:
            return p
    return n


def kernel(x, ln_g, ln_b, ffn1_w_in, ffn1_w_out, w_in, pool_w, pool_scale, swa_sinks, mlstm_conv, mlstm_i_bias, mlstm_f_bias, fox_f_bias, w_branch, w_gate, w_out, ffn2_w_in, ffn2_w_out):
    B, S, D = x.shape
    depth = ln_g.shape[0]
    assert B == 1 and w_in.shape[2] == 4880 and S % ML_CHUNK == 0
    alpha = float((2 * depth) ** 0.25)
    cos_t, sin_t = _rope_tables(S)
    tm = _pick(S, (512, 256))
    tm_ffn = _pick(S, (1024, 512, 256))
    t_mix = _pick(S, (512, 256))
    tq = _pick(S, (512, 256))

    g = SWA_HEADS // SWA_KV_HEADS
    wi = w_in.astype(BF16)
    hh = HEAD_DIM // 2
    swa_q = (wi[..., 512:1024].reshape(depth, D, SWA_KV_HEADS, g, 2, hh)
             .transpose(0, 1, 3, 4, 2, 5).reshape(depth, D, MIX_WIDTH))
    swa_k = (wi[..., 1024:1152].reshape(depth, D, SWA_KV_HEADS, 2, hh)
             .swapaxes(2, 3).reshape(depth, D, SWA_KV_HEADS * HEAD_DIM))
    wz = jnp.concatenate([
        wi[..., 0:512], swa_q, wi[..., 1280:3328], wi[..., 3336:4872],
        swa_k, wi[..., 1152:1280], wi[..., 3328:3336], wi[..., 4872:4880],
        jnp.zeros((depth, D, Z_COLS - COL_GATES - N_GATES), BF16)], axis=2)
    wbr = w_branch.astype(BF16)
    swa_rows = wbr[:, 1].reshape(depth, SWA_KV_HEADS, g, HEAD_DIM, D).swapaxes(1, 2).reshape(depth, 1, MIX_WIDTH, D)
    wb = jnp.concatenate([wbr[:, :1], swa_rows, wbr[:, 2:]], axis=1)
    wg = w_gate.astype(BF16)
    wo = w_out.astype(BF16)
    f1_in, f1_out = ffn1_w_in.astype(BF16), ffn1_w_out.astype(BF16)
    f2_in, f2_out = ffn2_w_in.astype(BF16), ffn2_w_out.astype(BF16)

    h = x.reshape(S, D)
    for l in range(depth):
        gate_bias = jnp.concatenate([mlstm_i_bias[l], mlstm_f_bias[l], fox_f_bias[l]]).reshape(1, N_GATES)
        sinks = swa_sinks[l]

        h, hb = _ffn(h, f1_in, f1_out, l, ln_g[l, 0], ln_b[l, 0], alpha=alpha, tm=tm_ffn, tf=512)
        z, zb = _inproj(hb, wz, l, tm=tm, tn=Z_COLS // 2)
        gates_col = _gates(z, gate_bias, T=ML_CHUNK)
        gates_row = gates_col.T
        nh = FOX_HEADS // 2
        c_col = gates_col[:, 2 * MLSTM_HEADS:].reshape(S, nh, 2).transpose(1, 0, 2)

        ya = _pool(z, pool_w[l].astype(BF16), pool_scale[l], T=t_mix)
        yb = _swa(z, sinks, cos_t, sin_t, T=t_mix)
        yc = _mlstm(z, mlstm_conv[l], gates_col, gates_row, L=ML_CHUNK)
        c_row = gates_row[2 * MLSTM_HEADS:].reshape(nh, 2, S)
        yd = _fox(z, zb, c_col, c_row, t=tq)
        merged = _merge(hb, (ya, yb, yc, yd), wg, wb, l, tm=_pick(S, (1024, 512, 256)), tn=512)
        h = _outproj(merged, h, wo, l, ln_g[l, 1], ln_b[l, 1], alpha=alpha, tm=tm)
        h, hb = _ffn(h, f2_in, f2_out, l, ln_g[l, 2], ln_b[l, 2], alpha=alpha, tm=tm_ffn, tf=512)
    return h.reshape(B, S, D)
```

```python
import functools

import jax
import jax.numpy as jnp
import numpy as np
from jax import lax
from jax.experimental import pallas as pl
from jax.experimental.pallas import tpu as pltpu

F32 = jnp.float32
BF16 = jnp.bfloat16

MIX_WIDTH = 512
HEAD_DIM = 64
POOL_WINDOWS = (2, 4, 8, 16)
POOL_CH = 128
SWA_HEADS = 8
SWA_KV_HEADS = 2
SWA_BLOCK = 128
ROPE_THETA = 10000.0
MLSTM_HEADS = 4
MLSTM_HEAD_DIM = 128
MLSTM_CONV = 4
FOX_HEADS = 8
LN_EPS = 1e-5
FFN_HALF = 0.5
LOG2E = 1.4426950408889634

LANES = 128
VMEM_LIMIT = 60 * 1024 * 1024

COL_POOL = 0
COL_SWA_Q = 512
COL_ML_Q = 1024
COL_ML_K = 1536
COL_ML_V = 2048
COL_ML_O = 2560
COL_FOX_Q = 3072
COL_FOX_K = 3584
COL_FOX_V = 4096
COL_SWA_K = 4608
COL_SWA_V = 4736
COL_GATES = 4864
Z_COLS = 5120
N_GATES = 16

ML_CHUNK = 256


def _cparams(sem):
    return pltpu.CompilerParams(dimension_semantics=sem, vmem_limit_bytes=VMEM_LIMIT)


def _layer_norm(y, g, b):
    mu = jnp.mean(y, axis=-1, keepdims=True)
    d = y - mu
    var = jnp.mean(d * d, axis=-1, keepdims=True)
    return d * lax.rsqrt(var + LN_EPS) * g + b


def _sigmoid(x):
    return 0.5 * jnp.tanh(0.5 * x) + 0.5


def _silu(x):
    h = 0.5 * x
    return h * jnp.tanh(h) + h


def _dot(a, b):
    return jnp.dot(a, b, preferred_element_type=F32)


def _dot_nt(a, b):
    return lax.dot_general(a, b, (((1,), (1,)), ((), ())), preferred_element_type=F32)


def _dot_tn(a, b):
    return lax.dot_general(a, b, (((0,), (0,)), ((), ())), preferred_element_type=F32)


def _ffn_kernel(x_ref, wg_ref, wu_ref, wo_ref, g_ref, b_ref, o_ref, ob_ref, acc_ref, *, alpha, nf):
    f = pl.program_id(1)
    tm = x_ref.shape[0]
    half = tm // 2
    rows = 64

    def layer_norm_rows(r0, n):
        g, b = g_ref[...], b_ref[...]
        for c in range(n // rows):
            rs = slice(r0 + c * rows, r0 + (c + 1) * rows)
            y = alpha * x_ref[rs, :] + FFN_HALF * acc_ref[rs, :]
            o = _layer_norm(y, g, b)
            o_ref[rs, :] = o
            ob_ref[rs, :] = o.astype(BF16)

    def step(first, last):
        for r in range(2):
            rs = slice(r * half, (r + 1) * half)
            xb = x_ref[rs, :].astype(BF16)
            gate = _dot(xb, wg_ref[...])
            up = _dot(xb, wu_ref[...])
            act = (_silu(gate) * up).astype(BF16)
            nw = acc_ref.shape[1] // 2
            for n in range(2):
                cols = slice(n * nw, (n + 1) * nw)
                d = _dot(act, wo_ref[:, cols])
                if first:
                    acc_ref[rs, cols] = d
                else:
                    acc_ref[rs, cols] += d
            if last:
                layer_norm_rows(r * half, half)

    if nf == 1:
        step(True, True)
    else:
        pl.when(f == 0)(lambda: step(True, False))
        pl.when((f > 0) & (f < nf - 1))(lambda: step(False, False))
        pl.when(f == nf - 1)(lambda: step(False, True))


def _ffn(x, w_in, w_out, layer, g, b, *, alpha, tm, tf):
    S, D = x.shape
    DF = w_out.shape[1]
    nf = DF // tf
    assert S % tm == 0 and DF % tf == 0
    return pl.pallas_call(
        functools.partial(_ffn_kernel, alpha=alpha, nf=nf),
        grid=(S // tm, nf),
        in_specs=[
            pl.BlockSpec((tm, D), lambda i, f: (i, 0)),
            pl.BlockSpec((None, D, tf), lambda i, f: (layer, 0, f)),
            pl.BlockSpec((None, D, tf), lambda i, f: (layer, 0, f + nf)),
            pl.BlockSpec((None, tf, D), lambda i, f: (layer, f, 0)),
            pl.BlockSpec((1, D), lambda i, f: (0, 0)),
            pl.BlockSpec((1, D), lambda i, f: (0, 0)),
        ],
        out_specs=[pl.BlockSpec((tm, D), lambda i, f: (i, 0), pipeline_mode=pl.Buffered(1)),
                   pl.BlockSpec((tm, D), lambda i, f: (i, 0), pipeline_mode=pl.Buffered(1))],
        out_shape=[jax.ShapeDtypeStruct((S, D), F32), jax.ShapeDtypeStruct((S, D), BF16)],
        scratch_shapes=[pltpu.VMEM((tm, D), F32)],
        compiler_params=_cparams(("parallel", "arbitrary")),
        name="ffn",
    )(x, w_in, w_in, w_out, g.reshape(1, D), b.reshape(1, D))


def _inproj_kernel(x_ref, w_ref, o_ref, ob_ref):
    z = _dot(x_ref[...], w_ref[...])
    o_ref[...] = z
    ob_ref[...] = z.astype(BF16)


def _inproj(xb, w, layer, *, tm, tn):
    S, D = xb.shape
    N = w.shape[2]
    assert S % tm == 0 and N % tn == 0
    ospec = pl.BlockSpec((tm, tn), lambda j, i: (i, j))
    return pl.pallas_call(
        _inproj_kernel,
        grid=(N // tn, S // tm),
        in_specs=[pl.BlockSpec((tm, D), lambda j, i: (i, 0)),
                  pl.BlockSpec((None, D, tn), lambda j, i: (layer, 0, j))],
        out_specs=[ospec, ospec],
        out_shape=[jax.ShapeDtypeStruct((S, N), F32), jax.ShapeDtypeStruct((S, N), BF16)],
        compiler_params=_cparams(("parallel", "arbitrary")),
        name="inproj",
    )(xb, w)


def _split3(x):
    hi = x.astype(BF16)
    r1 = x - hi.astype(F32)
    mid = r1.astype(BF16)
    lo = (r1 - mid.astype(F32)).astype(BF16)
    return hi, mid, lo


def _gates_kernel(z_ref, bias_ref, o_ref, carry_ref):
    c = pl.program_id(0)
    T = z_ref.shape[0]

    @pl.when(c == 0)
    def _():
        carry_ref[...] = jnp.zeros_like(carry_ref)

    pre = z_ref[:, 0:N_GATES] + bias_ref[...]
    lf = jax.nn.log_sigmoid(pre)
    row = lax.broadcasted_iota(jnp.int32, (T, T), 0)
    col = lax.broadcasted_iota(jnp.int32, (T, T), 1)
    tri = jnp.where(col <= row, 1.0, 0.0).astype(BF16)
    hi, mid, lo = _split3(lf)
    cs = _dot(tri, hi) + _dot(tri, mid) + _dot(tri, lo)
    lane = lax.broadcasted_iota(jnp.int32, (T, N_GATES), 1)
    is_fox = lane >= 2 * MLSTM_HEADS
    cs = cs + jnp.where(is_fox, carry_ref[...], 0.0)
    carry_ref[...] = cs[T - 1:T, :]
    o_ref[...] = jnp.where(lane < MLSTM_HEADS, pre, cs)


def _gates(z, bias, *, T):
    S = z.shape[0]
    return pl.pallas_call(
        _gates_kernel,
        grid=(S // T,),
        in_specs=[pl.BlockSpec((T, LANES), lambda c: (c, COL_GATES // LANES)),
                  pl.BlockSpec((1, N_GATES), lambda c: (0, 0))],
        out_specs=pl.BlockSpec((T, N_GATES), lambda c: (c, 0)),
        out_shape=jax.ShapeDtypeStruct((S, N_GATES), F32),
        scratch_shapes=[pltpu.VMEM((1, N_GATES), F32)],
        compiler_params=_cparams(("arbitrary",)),
        name="gates",
    )(z, bias)


POOL_HALO = 16


def _pool_kernel(u_ref, h_ref, w_ref, s_ref, o_ref):
    i = pl.program_id(0)
    T = u_ref.shape[0]
    u = u_ref[...]
    halo = jnp.where(i > 0, h_ref[...], 0.0)
    full = jnp.concatenate([halo, u], axis=0)
    t = i * T + lax.broadcasted_iota(jnp.int32, (T, 1), 0) + 1
    outs = []
    for g, win in enumerate(POOL_WINDOWS):
        cols = slice(g * POOL_CH, (g + 1) * POOL_CH)
        fg = full[:, cols]
        acc = fg[POOL_HALO:POOL_HALO + T]
        for j in range(1, win):
            acc = acc + fg[POOL_HALO - j:POOL_HALO - j + T]
        count = jnp.minimum(t, win).astype(F32)
        pooled = acc / count - u[:, cols]
        outs.append(_dot(pooled.astype(BF16), w_ref[g]))
    y = jnp.concatenate(outs, axis=1) * s_ref[...]
    o_ref[...] = y.astype(BF16)


def _pool(z, w, scale, *, T):
    S = z.shape[0]
    r = T // POOL_HALO
    return pl.pallas_call(
        _pool_kernel,
        grid=(S // T,),
        in_specs=[
            pl.BlockSpec((T, MIX_WIDTH), lambda i: (i, 0)),
            pl.BlockSpec((POOL_HALO, MIX_WIDTH), lambda i: (jnp.maximum(i * r - 1, 0), 0)),
            pl.BlockSpec((len(POOL_WINDOWS), POOL_CH, POOL_CH), lambda i: (0, 0, 0)),
            pl.BlockSpec((1, MIX_WIDTH), lambda i: (0, 0)),
        ],
        out_specs=pl.BlockSpec((T, MIX_WIDTH), lambda i: (i, 0)),
        out_shape=jax.ShapeDtypeStruct((S, MIX_WIDTH), BF16),
        compiler_params=_cparams(("parallel",)),
        name="pool",
    )(z, z, w, scale.reshape(1, MIX_WIDTH))


def _rope(x, cos, sin_signed):
    n = x.shape[1] // LANES
    outs = []
    for s in range(n):
        xs = x[:, s * LANES:(s + 1) * LANES]
        outs.append(xs * cos + pltpu.roll(xs, LANES // 2, axis=1) * sin_signed)
    return outs[0] if n == 1 else jnp.concatenate(outs, axis=1)


def _swa_kernel(sink_ref, q_ref, k_ref, v_ref, kh_ref, vh_ref, cos_ref, sin_ref, cosh_ref, sinh_ref, o_ref, *, nblk):
    i = pl.program_id(0)
    B = SWA_BLOCK
    G = SWA_HEADS // SWA_KV_HEADS
    cos = cos_ref[...]
    sin = sin_ref[...]
    q = (_rope(q_ref[...], cos, sin) * (HEAD_DIM ** -0.5)).astype(BF16)
    k = jnp.concatenate([_rope(kh_ref[...], cosh_ref[...], sinh_ref[...]), _rope(k_ref[...], cos, sin)],
                        axis=0).astype(BF16)
    v = jnp.concatenate([vh_ref[...], v_ref[...]], axis=0).astype(BF16)

    lane = lax.broadcasted_iota(jnp.int32, (G * B, LANES), 1)
    low = lane < HEAD_DIM
    low_qk = (lane % HEAD_DIM) < (HEAD_DIM // 2)
    low_v = lax.broadcasted_iota(jnp.int32, (2 * B, LANES), 1) < HEAD_DIM
    rowq = lax.broadcasted_iota(jnp.int32, (G * B, 2 * B), 0) % B
    colk = lax.broadcasted_iota(jnp.int32, (G * B, 2 * B), 1)
    band = (colk > rowq) & (colk <= rowq + B)
    grp = lax.broadcasted_iota(jnp.int32, (G * B, LANES), 0) // B
    sinks = []
    for kvh in range(SWA_KV_HEADS):
        col = jnp.zeros((G * B, LANES), F32)
        for g in range(G):
            col = jnp.where(grp == g, sink_ref[kvh * G + g], col)
        sinks.append(col)

    for blk in range(nblk):
        r0 = blk * B
        qb = jnp.concatenate([q[r0:r0 + B, s * LANES:(s + 1) * LANES] for s in range(G)], axis=0)
        kb = k[r0:r0 + 2 * B]
        vb = v[r0:r0 + 2 * B]
        valid = band
        if blk == 0:
            valid = band & ((colk >= B) | (i > 0))
        pvs = []
        for kvh in range(SWA_KV_HEADS):
            qm = jnp.where(low_qk if kvh == 0 else ~low_qk, qb, jnp.zeros_like(qb))
            s = _dot_nt(qm, kb)
            s = jnp.where(valid, s, -jnp.inf)
            sc = [s[:, c * LANES:(c + 1) * LANES] for c in range(2 * B // LANES)]
            mx = sc[0]
            for x in sc[1:]:
                mx = jnp.maximum(mx, x)
            m = jnp.maximum(jnp.max(mx, axis=1, keepdims=True), sinks[kvh])
            p = jnp.concatenate([jnp.exp(x - m).astype(BF16) for x in sc], axis=1)
            pv = _dot(p, jnp.where(low_v if kvh == 0 else ~low_v, vb, jnp.ones_like(vb)))
            den = pltpu.roll(pv, HEAD_DIM, axis=1) + jnp.exp(sinks[kvh] - m)
            pvs.append(pv / den)
        o = jnp.where(low, pvs[0], pvs[1])
        for s_ in range(G):
            o_ref[r0:r0 + B, s_ * LANES:(s_ + 1) * LANES] = o[s_ * B:(s_ + 1) * B].astype(BF16)


def _swa(z, sinks, cos, sin, *, T):
    S = z.shape[0]
    nblk = T // SWA_BLOCK
    cq, ck, cv = COL_SWA_Q // MIX_WIDTH, COL_SWA_K // LANES, COL_SWA_V // LANES
    halo = lambda i: jnp.maximum(i * nblk - 1, 0)
    return pl.pallas_call(
        functools.partial(_swa_kernel, nblk=nblk),
        grid=(S // T,),
        in_specs=[
            pl.BlockSpec(memory_space=pltpu.SMEM),
            pl.BlockSpec((T, MIX_WIDTH), lambda i: (i, cq)),
            pl.BlockSpec((T, LANES), lambda i: (i, ck)),
            pl.BlockSpec((T, LANES), lambda i: (i, cv)),
            pl.BlockSpec((SWA_BLOCK, LANES), lambda i: (halo(i), ck)),
            pl.BlockSpec((SWA_BLOCK, LANES), lambda i: (halo(i), cv)),
            pl.BlockSpec((T, LANES), lambda i: (i, 0)),
            pl.BlockSpec((T, LANES), lambda i: (i, 0)),
            pl.BlockSpec((SWA_BLOCK, LANES), lambda i: (halo(i), 0)),
            pl.BlockSpec((SWA_BLOCK, LANES), lambda i: (halo(i), 0)),
        ],
        out_specs=pl.BlockSpec((T, MIX_WIDTH), lambda i: (i, 0)),
        out_shape=jax.ShapeDtypeStruct((S, MIX_WIDTH), BF16),
        compiler_params=_cparams(("parallel",)),
        name="swa",
    )(sinks, z, z, z, z, z, cos, sin, cos, sin)


CONV_HALO = 8


def _mlstm_kernel(q_ref, k_ref, v_ref, og_ref, qh_ref, kh_ref, cw_ref, gc_ref, gr_ref, o_ref, c_sc, n_sc, m_sc):
    c = pl.program_id(0)
    L = q_ref.shape[0]
    H, dh = MLSTM_HEADS, MLSTM_HEAD_DIM

    @pl.when(c == 0)
    def _():
        c_sc[...] = jnp.zeros_like(c_sc)
        n_sc[...] = jnp.zeros_like(n_sc)
        m_sc[...] = jnp.zeros_like(m_sc)

    def conv_silu(x_ref, halo_ref, w):
        halo = jnp.where(c > 0, halo_ref[...], 0.0)
        full = jnp.concatenate([halo, x_ref[...]], axis=0)
        acc = jnp.zeros((L, x_ref.shape[1]), F32)
        for j in range(MLSTM_CONV):
            off = CONV_HALO - (MLSTM_CONV - 1) + j
            acc = acc + w[j:j + 1, :] * full[off:off + L]
        return _silu(acc)

    cw = cw_ref[...]
    qc = conv_silu(q_ref, qh_ref, cw[:, :H * dh]) * (dh ** -0.5)
    kc = conv_silu(k_ref, kh_ref, cw[:, H * dh:])
    vv = v_ref[...]
    gc = gc_ref[...]
    gr = gr_ref[...]
    row = lax.broadcasted_iota(jnp.int32, (L, L), 0)
    col = lax.broadcasted_iota(jnp.int32, (L, L), 1)
    causal = col <= row

    for h in range(H):
        sl = slice(h * dh, (h + 1) * dh)
        qf = qc[:, sl]
        qh, kh, vh = qf.astype(BF16), kc[:, sl].astype(BF16), vv[:, sl].astype(BF16)
        i_col = gc[:, h:h + 1]
        b_col = gc[:, H + h:H + h + 1]
        a_row = gr[h:h + 1, :] - gr[H + h:H + h + 1, :]
        m_prev = m_sc[h][0:1, 0:1]
        c_prev = c_sc[h]
        n_prev = n_sc[h]

        dlog = jnp.where(causal, b_col + a_row, -jnp.inf)
        inter = b_col + m_prev
        m_comb = jnp.maximum(inter, jnp.max(dlog, axis=1, keepdims=True))
        sm = _dot_nt(qh, kh) * jnp.exp(dlog - m_comb)
        w_inter = jnp.exp(inter - m_comb)
        num = _dot(sm.astype(BF16), vh) + w_inter * _dot_nt(qh, c_prev.astype(BF16))
        den = jnp.sum(sm, axis=1, keepdims=True) + w_inter * jnp.sum(qf * n_prev, axis=1, keepdims=True)
        hid = num / jnp.maximum(jnp.abs(den), jnp.exp(-m_comb))
        o_ref[:, sl] = (_sigmoid(og_ref[:, sl]) * hid).astype(BF16)

        bl = b_col[L - 1:L, :]
        g = bl - b_col + i_col
        m_new = jnp.maximum(bl + m_prev, jnp.max(g, axis=0, keepdims=True))
        wk = jnp.exp(g - m_new)
        decay = jnp.exp(bl + m_prev - m_new)
        c_sc[h] = decay * c_prev + _dot_tn((wk * vv[:, sl]).astype(BF16), kh)
        n_sc[h] = decay * n_prev + jnp.sum(wk * kc[:, sl], axis=0, keepdims=True)
        m_sc[h] = jnp.broadcast_to(m_new, m_sc.shape[1:])


def _mlstm(z, conv_w, gates_col, gates_row, *, L):
    S = z.shape[0]
    r = L // CONV_HALO
    W = MLSTM_HEADS * MLSTM_HEAD_DIM
    halo = lambda c: jnp.maximum(c * r - 1, 0)
    return pl.pallas_call(
        _mlstm_kernel,
        grid=(S // L,),
        in_specs=[
            pl.BlockSpec((L, W), lambda c: (c, COL_ML_Q // W)),
            pl.BlockSpec((L, W), lambda c: (c, COL_ML_K // W)),
            pl.BlockSpec((L, W), lambda c: (c, COL_ML_V // W)),
            pl.BlockSpec((L, W), lambda c: (c, COL_ML_O // W)),
            pl.BlockSpec((CONV_HALO, W), lambda c: (halo(c), COL_ML_Q // W)),
            pl.BlockSpec((CONV_HALO, W), lambda c: (halo(c), COL_ML_K // W)),
            pl.BlockSpec((MLSTM_CONV, 2 * W), lambda c: (0, 0)),
            pl.BlockSpec((L, N_GATES), lambda c: (c, 0)),
            pl.BlockSpec((N_GATES, L), lambda c: (0, c)),
        ],
        out_specs=pl.BlockSpec((L, W), lambda c: (c, 0)),
        out_shape=jax.ShapeDtypeStruct((S, W), BF16),
        scratch_shapes=[pltpu.VMEM((MLSTM_HEADS, MLSTM_HEAD_DIM, MLSTM_HEAD_DIM), F32),
                        pltpu.VMEM((MLSTM_HEADS, 1, MLSTM_HEAD_DIM), F32),
                        pltpu.VMEM((MLSTM_HEADS, 8, LANES), F32)],
        compiler_params=_cparams(("arbitrary",)),
        name="mlstm",
    )(z, z, z, z, z, z, conv_w, gates_col, gates_row)


def _fox_kernel(q_ref, k_ref, v_ref, cq_ref, ck_ref, o_ref, m_sc, acc_sc, s_sc, *, t):
    i = pl.program_id(1)
    nch = t // LANES
    m_sc[...] = jnp.full_like(m_sc, -jnp.inf)
    acc_sc[...] = jnp.zeros_like(acc_sc)

    q = q_ref[...] * (LOG2E * HEAD_DIM ** -0.5)
    lane = lax.broadcasted_iota(jnp.int32, (t, LANES), 1)
    row = lax.broadcasted_iota(jnp.int32, (t, LANES), 0)
    low = lane < HEAD_DIM
    own = [low, ~low]
    qa = [jnp.where(own[a], q, 0.0).astype(BF16) for a in range(2)]
    cq = cq_ref[...] * LOG2E
    ca = [jnp.broadcast_to(cq[:, a:a + 1], (t, LANES)) for a in range(2)]

    def scores(j, slot):
        j0 = pl.multiple_of(j * t, t)
        kb = k_ref[pl.ds(j0, t), :]
        for a in range(2):
            s_sc[slot, a] = _dot_nt(qa[a], kb)

    def softmax_pv(j, slot, masked):
        j0 = pl.multiple_of(j * t, t)
        vb = v_ref[pl.ds(j0, t), :]
        ck = ck_ref[:, pl.ds(j0, t)] * LOG2E
        ps, alphas = [], []
        for a in range(2):
            sc = []
            for c in range(nch):
                x = s_sc[slot, a, :, c * LANES:(c + 1) * LANES] - ck[a:a + 1, c * LANES:(c + 1) * LANES]
                if masked:
                    x = jnp.where(lane + c * LANES <= row, x, -jnp.inf)
                sc.append(x)
            mx = sc[0]
            for c in range(1, nch):
                mx = jnp.maximum(mx, sc[c])
            m_old = m_sc[a]
            m_new = jnp.maximum(m_old, jnp.max(mx, axis=1, keepdims=True) + ca[a])
            alpha = jnp.exp2(m_old - m_new)
            mm = m_new - ca[a]
            ps += [jnp.exp2(x - mm).astype(BF16) for x in sc]
            alphas.append(alpha)
            m_sc[a] = m_new
        one, zero = jnp.ones_like(vb), jnp.zeros_like(vb)
        vblk = jnp.concatenate([jnp.concatenate([jnp.where(low, vb, one), zero], axis=1),
                                jnp.concatenate([zero, jnp.where(low, one, vb)], axis=1)], axis=0)
        acc_sc[...] = jnp.concatenate(alphas, axis=1) * acc_sc[...] + _dot(jnp.concatenate(ps, axis=1), vblk)

    scores(0, 0)

    def pair(j):
        scores(j + 1, 1)
        softmax_pv(j, 0, False)
        scores(j + 2, 0)
        softmax_pv(j + 1, 1, False)

    def body(j4, carry):
        pair(4 * j4)
        pair(4 * j4 + 2)
        return carry

    lax.fori_loop(0, i // 4, body, 0)

    @pl.when(i % 4 >= 2)
    def _():
        pair(4 * (i // 4))

    @pl.when(i % 2 == 1)
    def _():
        scores(i, 1)
        softmax_pv(i - 1, 0, False)
        softmax_pv(i, 1, True)

    @pl.when(i % 2 == 0)
    def _():
        softmax_pv(i, 0, True)

    outs = []
    for a in range(2):
        acc = acc_sc[:, a * LANES:(a + 1) * LANES]
        outs.append(acc / pltpu.roll(acc, HEAD_DIM, axis=1))
    o_ref[...] = jnp.where(low, outs[0], outs[1]).astype(BF16)


def _fox(z, zb, c_col, c_row, *, t):
    S = z.shape[0]
    npair = FOX_HEADS // 2
    cq0, ck0, cv0 = COL_FOX_Q // LANES, COL_FOX_K // LANES, COL_FOX_V // LANES
    return pl.pallas_call(
        functools.partial(_fox_kernel, t=t),
        grid=(npair, S // t),
        in_specs=[
            pl.BlockSpec((t, LANES), lambda p, i: (i, cq0 + p)),
            pl.BlockSpec((S, LANES), lambda p, i: (0, ck0 + p)),
            pl.BlockSpec((S, LANES), lambda p, i: (0, cv0 + p)),
            pl.BlockSpec((None, t, 2), lambda p, i: (p, i, 0)),
            pl.BlockSpec((None, 2, S), lambda p, i: (p, 0, 0)),
        ],
        out_specs=pl.BlockSpec((t, LANES), lambda p, i: (i, p)),
        out_shape=jax.ShapeDtypeStruct((S, MIX_WIDTH), BF16),
        scratch_shapes=[pltpu.VMEM((2, t, LANES), F32), pltpu.VMEM((t, 2 * LANES), F32),
                        pltpu.VMEM((2, 2, t, t), F32)],
        compiler_params=_cparams(("parallel", "arbitrary")),
        name="fox",
    )(z, zb, zb, c_col, c_row)


def _merge_kernel(hb_ref, ya_ref, yb_ref, yc_ref, yd_ref, wg_ref, wb_ref, o_ref):
    hb = hb_ref[...]
    acc = None
    for b, y_ref in enumerate((ya_ref, yb_ref, yc_ref, yd_ref)):
        gate = _sigmoid(_dot(hb, wg_ref[b]))
        term = gate * _dot(y_ref[...], wb_ref[b])
        acc = term if acc is None else acc + term
    o_ref[...] = acc.astype(BF16)


def _merge(hb, ys, wg, wb, layer, *, tm, tn):
    S, D = hb.shape
    _, nb, W, _ = wb.shape
    yspec = pl.BlockSpec((tm, W), lambda n, i: (i, 0))
    return pl.pallas_call(
        _merge_kernel,
        grid=(D // tn, S // tm),
        in_specs=[pl.BlockSpec((tm, D), lambda n, i: (i, 0)), yspec, yspec, yspec, yspec,
                  pl.BlockSpec((None, nb, D, tn), lambda n, i: (layer, 0, 0, n)),
                  pl.BlockSpec((None, nb, W, tn), lambda n, i: (layer, 0, 0, n))],
        out_specs=pl.BlockSpec((tm, tn), lambda n, i: (i, n)),
        out_shape=jax.ShapeDtypeStruct((S, D), BF16),
        compiler_params=_cparams(("parallel", "arbitrary")),
        name="merge",
    )(hb, *ys, wg, wb)


def _outproj_kernel(m_ref, h_ref, w_ref, g_ref, b_ref, o_ref, *, alpha):
    g, b = g_ref[...], b_ref[...]
    rows = 128
    for r in range(m_ref.shape[0] // rows):
        rs = slice(r * rows, (r + 1) * rows)
        y = alpha * h_ref[rs, :] + _dot(m_ref[rs, :], w_ref[...])
        o_ref[rs, :] = _layer_norm(y, g, b)


def _outproj(merged, h, w, layer, g, b, *, alpha, tm):
    S, D = h.shape
    return pl.pallas_call(
        functools.partial(_outproj_kernel, alpha=alpha),
        grid=(S // tm,),
        in_specs=[pl.BlockSpec((tm, D), lambda i: (i, 0)), pl.BlockSpec((tm, D), lambda i: (i, 0)),
                  pl.BlockSpec((None, D, D), lambda i: (layer, 0, 0)),
                  pl.BlockSpec((1, D), lambda i: (0, 0)), pl.BlockSpec((1, D), lambda i: (0, 0))],
        out_specs=pl.BlockSpec((tm, D), lambda i: (i, 0)),
        out_shape=jax.ShapeDtypeStruct((S, D), F32),
        compiler_params=_cparams(("parallel",)),
        name="outproj",
    )(merged, h, w, g.reshape(1, D), b.reshape(1, D))


def _swa_head_order():
    g = SWA_HEADS // SWA_KV_HEADS
    order = []
    for s in range(g):
        order += [s, g + s]
    return order


def _swa_col_perm():
    return np.concatenate([np.arange(h * HEAD_DIM, (h + 1) * HEAD_DIM) for h in _swa_head_order()])


def _rope_tables(S):
    half = HEAD_DIM // 2
    inv = ROPE_THETA ** (-jnp.arange(half, dtype=F32) / half)
    ang = jnp.arange(S, dtype=jnp.int32).astype(F32)[:, None] * inv[None, :]
    cos, sin = jnp.cos(ang), jnp.sin(ang)
    cos_t = jnp.concatenate([cos, cos, cos, cos], axis=1)
    sin_t = jnp.concatenate([-sin, -sin, sin, sin], axis=1)
    return cos_t, sin_t


def _pick(n, prefs):
    for p in prefs:
        if n % p == 0:
            return p
    return n


def kernel(x, ln_g, ln_b, ffn1_w_in, ffn1_w_out, w_in, pool_w, pool_scale, swa_sinks, mlstm_conv, mlstm_i_bias, mlstm_f_bias, fox_f_bias, w_branch, w_gate, w_out, ffn2_w_in, ffn2_w_out):
    B, S, D = x.shape
    depth = ln_g.shape[0]
    assert B == 1 and w_in.shape[2] == 4880 and S % ML_CHUNK == 0
    alpha = float((2 * depth) ** 0.25)
    cos_t, sin_t = _rope_tables(S)
    tm = _pick(S, (512, 256))
    tm_ffn = _pick(S, (1024, 512, 256))
    t_mix = _pick(S, (512, 256))
    tq = _pick(S, (512, 256))

    g = SWA_HEADS // SWA_KV_HEADS
    wi = w_in.astype(BF16)
    hh = HEAD_DIM // 2
    swa_q = (wi[..., 512:1024].reshape(depth, D, SWA_KV_HEADS, g, 2, hh)
             .transpose(0, 1, 3, 4, 2, 5).reshape(depth, D, MIX_WIDTH))
    swa_k = (wi[..., 1024:1152].reshape(depth, D, SWA_KV_HEADS, 2, hh)
             .swapaxes(2, 3).reshape(depth, D, SWA_KV_HEADS * HEAD_DIM))
    wz = jnp.concatenate([
        wi[..., 0:512], swa_q, wi[..., 1280:3328], wi[..., 3336:4872],
        swa_k, wi[..., 1152:1280], wi[..., 3328:3336], wi[..., 4872:4880],
        jnp.zeros((depth, D, Z_COLS - COL_GATES - N_GATES), BF16)], axis=2)
    wbr = w_branch.astype(BF16)
    swa_rows = wbr[:, 1].reshape(depth, SWA_KV_HEADS, g, HEAD_DIM, D).swapaxes(1, 2).reshape(depth, 1, MIX_WIDTH, D)
    wb = jnp.concatenate([wbr[:, :1], swa_rows, wbr[:, 2:]], axis=1)
    wg = w_gate.astype(BF16)
    wo = w_out.astype(BF16)
    f1_in, f1_out = ffn1_w_in.astype(BF16), ffn1_w_out.astype(BF16)
    f2_in, f2_out = ffn2_w_in.astype(BF16), ffn2_w_out.astype(BF16)

    h = x.reshape(S, D)
    for l in range(depth):
        gate_bias = jnp.concatenate([mlstm_i_bias[l], mlstm_f_bias[l], fox_f_bias[l]]).reshape(1, N_GATES)
        sinks = swa_sinks[l]

        h, hb = _ffn(h, f1_in, f1_out, l, ln_g[l, 0], ln_b[l, 0], alpha=alpha, tm=tm_ffn, tf=512)
        z, zb = _inproj(hb, wz, l, tm=tm, tn=Z_COLS // 2)
        gates_col = _gates(z, gate_bias, T=ML_CHUNK)
        gates_row = gates_col.T
        nh = FOX_HEADS // 2
        c_col = gates_col[:, 2 * MLSTM_HEADS:].reshape(S, nh, 2).transpose(1, 0, 2)

        ya = _pool(z, pool_w[l].astype(BF16), pool_scale[l], T=t_mix)
        yb = _swa(z, sinks, cos_t, sin_t, T=t_mix)
        yc = _mlstm(z, mlstm_conv[l], gates_col, gates_row, L=ML_CHUNK)
        c_row = gates_row[2 * MLSTM_HEADS:].reshape(nh, 2, S)
        yd = _fox(z, zb, c_col, c_row, t=tq)
        merged = _merge(hb, (ya, yb, yc, yd), wg, wb, l, tm=_pick(S, (1024, 512, 256)), tn=512)
        h = _outproj(merged, h, wo, l, ln_g[l, 1], ln_b[l, 1], alpha=alpha, tm=tm)
        h, hb = _ffn(h, f2_in, f2_out, l, ln_g[l, 2], ln_b[l, 2], alpha=alpha, tm=tm_ffn, tf=512)
    return h.reshape(B, S, D)
```

```python
import functools

import jax
import jax.numpy as jnp
from jax import lax
from jax.experimental import pallas as pl
from jax.experimental.pallas import tpu as pltpu

F32 = jnp.float32
BF16 = jnp.bfloat16

MIX_WIDTH = 512
HEAD_DIM = 64
POOL_WINDOWS = (2, 4, 8, 16)
POOL_CH = 128
SWA_HEADS = 8
SWA_KV_HEADS = 2
SWA_BLOCK = 128
ROPE_THETA = 10000.0
MLSTM_HEADS = 4
MLSTM_HEAD_DIM = 128
MLSTM_CONV = 4
FOX_HEADS = 8
LN_EPS = 1e-5
FFN_HALF = 0.5
LOG2E = 1.4426950408889634

LANES = 128
VMEM_LIMIT = 60 * 1024 * 1024

COL_POOL = 0
COL_SWA_Q = 512
COL_ML_Q = 1024
COL_ML_K = 1536
COL_ML_V = 2048
COL_ML_O = 2560
COL_FOX_Q = 3072
COL_FOX_K = 3584
COL_FOX_V = 4096
COL_SWA_K = 4608
COL_SWA_V = 4736
COL_GATES = 4864
Z_COLS = 5120
N_GATES = 16
FOX_COL0 = 2 * MLSTM_HEADS

ML_CHUNK = 256


def _cparams(sem):
    return pltpu.CompilerParams(dimension_semantics=sem, vmem_limit_bytes=VMEM_LIMIT)


def _layer_norm(y, g, b):
    mu = jnp.mean(y, axis=-1, keepdims=True)
    d = y - mu
    var = jnp.mean(d * d, axis=-1, keepdims=True)
    return d * lax.rsqrt(var + LN_EPS) * g + b


def _sigmoid(x):
    return 0.5 * jnp.tanh(0.5 * x) + 0.5


def _silu(x):
    h = 0.5 * x
    return h * jnp.tanh(h) + h


def _dot(a, b):
    return jnp.dot(a, b, preferred_element_type=F32)


def _dot_nt(a, b):
    return lax.dot_general(a, b, (((1,), (1,)), ((), ())), preferred_element_type=F32)


def _dot_tn(a, b):
    return lax.dot_general(a, b, (((0,), (0,)), ((), ())), preferred_element_type=F32)


def _ffn_kernel(x_ref, wg_ref, wu_ref, wo_ref, g_ref, b_ref, o_ref, ob_ref, acc_ref, *, alpha, nf):
    f = pl.program_id(1)
    tm = x_ref.shape[0]
    half = tm // 2
    rows = 64

    def layer_norm_rows(r0, n):
        g, b = g_ref[...], b_ref[...]
        for c in range(n // rows):
            rs = slice(r0 + c * rows, r0 + (c + 1) * rows)
            y = alpha * x_ref[rs, :] + FFN_HALF * acc_ref[rs, :]
            o = _layer_norm(y, g, b)
            o_ref[rs, :] = o
            ob_ref[rs, :] = o.astype(BF16)

    def step(first, last):
        for r in range(2):
            rs = slice(r * half, (r + 1) * half)
            xb = x_ref[rs, :].astype(BF16)
            gate = _dot(xb, wg_ref[...])
            up = _dot(xb, wu_ref[...])
            act = (_silu(gate) * up).astype(BF16)
            nw = acc_ref.shape[1] // 2
            for n in range(2):
                cols = slice(n * nw, (n + 1) * nw)
                d = _dot(act, wo_ref[:, cols])
                if first:
                    acc_ref[rs, cols] = d
                else:
                    acc_ref[rs, cols] += d
            if last:
                layer_norm_rows(r * half, half)

    if nf == 1:
        step(True, True)
    else:
        pl.when(f == 0)(lambda: step(True, False))
        pl.when((f > 0) & (f < nf - 1))(lambda: step(False, False))
        pl.when(f == nf - 1)(lambda: step(False, True))


def _ffn(x, w_in, w_out, layer, g, b, *, alpha, tm, tf):
    S, D = x.shape
    DF = w_out.shape[1]
    nf = DF // tf
    assert S % tm == 0 and DF % tf == 0
    return pl.pallas_call(
        functools.partial(_ffn_kernel, alpha=alpha, nf=nf),
        grid=(S // tm, nf),
        in_specs=[
            pl.BlockSpec((tm, D), lambda i, f: (i, 0)),
            pl.BlockSpec((None, D, tf), lambda i, f: (layer, 0, f)),
            pl.BlockSpec((None, D, tf), lambda i, f: (layer, 0, f + nf)),
            pl.BlockSpec((None, tf, D), lambda i, f: (layer, f, 0)),
            pl.BlockSpec((1, D), lambda i, f: (0, 0)),
            pl.BlockSpec((1, D), lambda i, f: (0, 0)),
        ],
        out_specs=[pl.BlockSpec((tm, D), lambda i, f: (i, 0), pipeline_mode=pl.Buffered(1)),
                   pl.BlockSpec((tm, D), lambda i, f: (i, 0), pipeline_mode=pl.Buffered(1))],
        out_shape=[jax.ShapeDtypeStruct((S, D), F32), jax.ShapeDtypeStruct((S, D), BF16)],
        scratch_shapes=[pltpu.VMEM((tm, D), F32)],
        compiler_params=_cparams(("parallel", "arbitrary")),
        name="ffn",
    )(x, w_in, w_in, w_out, g.reshape(1, D), b.reshape(1, D))


def _inproj_kernel(x_ref, w_ref, o_ref, ob_ref):
    z = _dot(x_ref[...], w_ref[...])
    o_ref[...] = z
    ob_ref[...] = z.astype(BF16)


def _inproj(xb, w, layer, *, tm, tn):
    S, D = xb.shape
    N = w.shape[2]
    assert S % tm == 0 and N % tn == 0
    ospec = pl.BlockSpec((tm, tn), lambda j, i: (i, j))
    return pl.pallas_call(
        _inproj_kernel,
        grid=(N // tn, S // tm),
        in_specs=[pl.BlockSpec((tm, D), lambda j, i: (i, 0)),
                  pl.BlockSpec((None, D, tn), lambda j, i: (layer, 0, j))],
        out_specs=[ospec, ospec],
        out_shape=[jax.ShapeDtypeStruct((S, N), F32), jax.ShapeDtypeStruct((S, N), BF16)],
        compiler_params=_cparams(("parallel", "arbitrary")),
        name="inproj",
    )(xb, w)


def _split3(x):
    hi = x.astype(BF16)
    r1 = x - hi.astype(F32)
    mid = r1.astype(BF16)
    lo = (r1 - mid.astype(F32)).astype(BF16)
    return hi, mid, lo


def _gates_kernel(z_ref, bias_ref, o_ref, carry_ref):
    c = pl.program_id(0)
    T = z_ref.shape[0]

    @pl.when(c == 0)
    def _():
        carry_ref[...] = jnp.zeros_like(carry_ref)

    pre = z_ref[:, 0:N_GATES] + bias_ref[...]
    lf = jax.nn.log_sigmoid(pre)
    row = lax.broadcasted_iota(jnp.int32, (T, T), 0)
    col = lax.broadcasted_iota(jnp.int32, (T, T), 1)
    tri = jnp.where(col <= row, 1.0, 0.0).astype(BF16)
    hi, mid, lo = _split3(lf)
    cs = _dot(tri, hi) + _dot(tri, mid) + _dot(tri, lo)
    lane = lax.broadcasted_iota(jnp.int32, (T, N_GATES), 1)
    is_fox = lane >= 2 * MLSTM_HEADS
    cs = cs + jnp.where(is_fox, carry_ref[...], 0.0)
    carry_ref[...] = cs[T - 1:T, :]
    o_ref[...] = jnp.where(lane < MLSTM_HEADS, pre, cs)


def _gates(z, bias, *, T):
    S = z.shape[0]
    return pl.pallas_call(
        _gates_kernel,
        grid=(S // T,),
        in_specs=[pl.BlockSpec((T, LANES), lambda c: (c, COL_GATES // LANES)),
                  pl.BlockSpec((1, N_GATES), lambda c: (0, 0))],
        out_specs=pl.BlockSpec((T, N_GATES), lambda c: (c, 0)),
        out_shape=jax.ShapeDtypeStruct((S, N_GATES), F32),
        scratch_shapes=[pltpu.VMEM((1, N_GATES), F32)],
        compiler_params=_cparams(("arbitrary",)),
        name="gates",
    )(z, bias)


POOL_HALO = 16


def _pool_kernel(u_ref, h_ref, w_ref, s_ref, o_ref):
    i = pl.program_id(0)
    T = u_ref.shape[0]
    u = u_ref[...]
    halo = jnp.where(i > 0, h_ref[...], 0.0)
    full = jnp.concatenate([halo, u], axis=0)
    t = i * T + lax.broadcasted_iota(jnp.int32, (T, 1), 0) + 1
    outs = []
    for g, win in enumerate(POOL_WINDOWS):
        cols = slice(g * POOL_CH, (g + 1) * POOL_CH)
        fg = full[:, cols]
        acc = fg[POOL_HALO:POOL_HALO + T]
        for j in range(1, win):
            acc = acc + fg[POOL_HALO - j:POOL_HALO - j + T]
        count = jnp.minimum(t, win).astype(F32)
        pooled = acc / count - u[:, cols]
        outs.append(_dot(pooled.astype(BF16), w_ref[g]))
    y = jnp.concatenate(outs, axis=1) * s_ref[...]
    o_ref[...] = y.astype(BF16)


def _pool(z, w, scale, *, T):
    S = z.shape[0]
    r = T // POOL_HALO
    return pl.pallas_call(
        _pool_kernel,
        grid=(S // T,),
        in_specs=[
            pl.BlockSpec((T, MIX_WIDTH), lambda i: (i, 0)),
            pl.BlockSpec((POOL_HALO, MIX_WIDTH), lambda i: (jnp.maximum(i * r - 1, 0), 0)),
            pl.BlockSpec((len(POOL_WINDOWS), POOL_CH, POOL_CH), lambda i: (0, 0, 0)),
            pl.BlockSpec((1, MIX_WIDTH), lambda i: (0, 0)),
        ],
        out_specs=pl.BlockSpec((T, MIX_WIDTH), lambda i: (i, 0)),
        out_shape=jax.ShapeDtypeStruct((S, MIX_WIDTH), BF16),
        compiler_params=_cparams(("parallel",)),
        name="pool",
    )(z, z, w, scale.reshape(1, MIX_WIDTH))


def _rope(x, cos, sin_signed):
    n = x.shape[1] // LANES
    outs = []
    for s in range(n):
        xs = x[:, s * LANES:(s + 1) * LANES]
        outs.append(xs * cos + pltpu.roll(xs, LANES // 2, axis=1) * sin_signed)
    return outs[0] if n == 1 else jnp.concatenate(outs, axis=1)


def _swa_kernel(sink_ref, q_ref, k_ref, v_ref, kh_ref, vh_ref, cos_ref, sin_ref, cosh_ref, sinh_ref, o_ref, *, nblk):
    i = pl.program_id(0)
    B = SWA_BLOCK
    G = SWA_HEADS // SWA_KV_HEADS
    cos = cos_ref[...]
    sin = sin_ref[...]
    q = (_rope(q_ref[...], cos, sin) * (HEAD_DIM ** -0.5)).astype(BF16)
    k = jnp.concatenate([_rope(kh_ref[...], cosh_ref[...], sinh_ref[...]), _rope(k_ref[...], cos, sin)],
                        axis=0).astype(BF16)
    v = jnp.concatenate([vh_ref[...], v_ref[...]], axis=0).astype(BF16)

    lane = lax.broadcasted_iota(jnp.int32, (G * B, LANES), 1)
    low = lane < HEAD_DIM
    low_qk = (lane % HEAD_DIM) < (HEAD_DIM // 2)
    low_v = lax.broadcasted_iota(jnp.int32, (2 * B, LANES), 1) < HEAD_DIM
    rowq = lax.broadcasted_iota(jnp.int32, (G * B, 2 * B), 0) % B
    colk = lax.broadcasted_iota(jnp.int32, (G * B, 2 * B), 1)
    band = (colk > rowq) & (colk <= rowq + B)
    grp = lax.broadcasted_iota(jnp.int32, (G * B, LANES), 0) // B
    sinks = []
    for kvh in range(SWA_KV_HEADS):
        col = jnp.zeros((G * B, LANES), F32)
        for g in range(G):
            col = jnp.where(grp == g, sink_ref[kvh * G + g], col)
        sinks.append(col)

    for blk in range(nblk):
        r0 = blk * B
        qb = jnp.concatenate([q[r0:r0 + B, s * LANES:(s + 1) * LANES] for s in range(G)], axis=0)
        kb = k[r0:r0 + 2 * B]
        vb = v[r0:r0 + 2 * B]
        valid = band
        if blk == 0:
            valid = band & ((colk >= B) | (i > 0))
        pvs = []
        for kvh in range(SWA_KV_HEADS):
            qm = jnp.where(low_qk if kvh == 0 else ~low_qk, qb, jnp.zeros_like(qb))
            s = _dot_nt(qm, kb)
            s = jnp.where(valid, s, -jnp.inf)
            sc = [s[:, c * LANES:(c + 1) * LANES] for c in range(2 * B // LANES)]
            mx = sc[0]
            for x in sc[1:]:
                mx = jnp.maximum(mx, x)
            m = jnp.maximum(jnp.max(mx, axis=1, keepdims=True), sinks[kvh])
            p = jnp.concatenate([jnp.exp(x - m).astype(BF16) for x in sc], axis=1)
            pv = _dot(p, jnp.where(low_v if kvh == 0 else ~low_v, vb, jnp.ones_like(vb)))
            den = pltpu.roll(pv, HEAD_DIM, axis=1) + jnp.exp(sinks[kvh] - m)
            pvs.append(pv / den)
        o = jnp.where(low, pvs[0], pvs[1])
        for s_ in range(G):
            o_ref[r0:r0 + B, s_ * LANES:(s_ + 1) * LANES] = o[s_ * B:(s_ + 1) * B].astype(BF16)


def _swa(z, sinks, cos, sin, *, T):
    S = z.shape[0]
    nblk = T // SWA_BLOCK
    cq, ck, cv = COL_SWA_Q // MIX_WIDTH, COL_SWA_K // LANES, COL_SWA_V // LANES
    halo = lambda i: jnp.maximum(i * nblk - 1, 0)
    return pl.pallas_call(
        functools.partial(_swa_kernel, nblk=nblk),
        grid=(S // T,),
        in_specs=[
            pl.BlockSpec(memory_space=pltpu.SMEM),
            pl.BlockSpec((T, MIX_WIDTH), lambda i: (i, cq)),
            pl.BlockSpec((T, LANES), lambda i: (i, ck)),
            pl.BlockSpec((T, LANES), lambda i: (i, cv)),
            pl.BlockSpec((SWA_BLOCK, LANES), lambda i: (halo(i), ck)),
            pl.BlockSpec((SWA_BLOCK, LANES), lambda i: (halo(i), cv)),
            pl.BlockSpec((T, LANES), lambda i: (i, 0)),
            pl.BlockSpec((T, LANES), lambda i: (i, 0)),
            pl.BlockSpec((SWA_BLOCK, LANES), lambda i: (halo(i), 0)),
            pl.BlockSpec((SWA_BLOCK, LANES), lambda i: (halo(i), 0)),
        ],
        out_specs=pl.BlockSpec((T, MIX_WIDTH), lambda i: (i, 0)),
        out_shape=jax.ShapeDtypeStruct((S, MIX_WIDTH), BF16),
        compiler_params=_cparams(("parallel",)),
        name="swa",
    )(sinks, z, z, z, z, z, cos, sin, cos, sin)


CONV_HALO = 8


def _mlstm_kernel(q_ref, k_ref, v_ref, og_ref, qh_ref, kh_ref, cw_ref, gc_ref, gr_ref, o_ref, c_sc, n_sc, m_sc):
    c = pl.program_id(0)
    L = q_ref.shape[0]
    H, dh = MLSTM_HEADS, MLSTM_HEAD_DIM

    @pl.when(c == 0)
    def _():
        c_sc[...] = jnp.zeros_like(c_sc)
        n_sc[...] = jnp.zeros_like(n_sc)
        m_sc[...] = jnp.zeros_like(m_sc)

    def conv_silu(x_ref, halo_ref, w):
        halo = jnp.where(c > 0, halo_ref[...], 0.0)
        full = jnp.concatenate([halo, x_ref[...]], axis=0)
        acc = jnp.zeros((L, x_ref.shape[1]), F32)
        for j in range(MLSTM_CONV):
            off = CONV_HALO - (MLSTM_CONV - 1) + j
            acc = acc + w[j:j + 1, :] * full[off:off + L]
        return _silu(acc)

    cw = cw_ref[...]
    qc = conv_silu(q_ref, qh_ref, cw[:, :H * dh]) * (dh ** -0.5)
    kc = conv_silu(k_ref, kh_ref, cw[:, H * dh:])
    vv = v_ref[...]
    gc = gc_ref[...]
    gr = gr_ref[...]
    row = lax.broadcasted_iota(jnp.int32, (L, L), 0)
    col = lax.broadcasted_iota(jnp.int32, (L, L), 1)
    causal = col <= row

    for h in range(H):
        sl = slice(h * dh, (h + 1) * dh)
        qf = qc[:, sl]
        qh, kh, vh = qf.astype(BF16), kc[:, sl].astype(BF16), vv[:, sl].astype(BF16)
        i_col = gc[:, h:h + 1]
        b_col = gc[:, H + h:H + h + 1]
        a_row = gr[h:h + 1, :] - gr[H + h:H + h + 1, :]
        m_prev = m_sc[h][0:1, 0:1]
        c_prev = c_sc[h]
        n_prev = n_sc[h]

        dlog = jnp.where(causal, b_col + a_row, -jnp.inf)
        inter = b_col + m_prev
        m_comb = jnp.maximum(inter, jnp.max(dlog, axis=1, keepdims=True))
        sm = _dot_nt(qh, kh) * jnp.exp(dlog - m_comb)
        w_inter = jnp.exp(inter - m_comb)
        num = _dot(sm.astype(BF16), vh) + w_inter * _dot_nt(qh, c_prev.astype(BF16))
        den = jnp.sum(sm, axis=1, keepdims=True) + w_inter * jnp.sum(qf * n_prev, axis=1, keepdims=True)
        hid = num / jnp.maximum(jnp.abs(den), jnp.exp(-m_comb))
        o_ref[:, sl] = (_sigmoid(og_ref[:, sl]) * hid).astype(BF16)

        bl = b_col[L - 1:L, :]
        g = bl - b_col + i_col
        m_new = jnp.maximum(bl + m_prev, jnp.max(g, axis=0, keepdims=True))
        wk = jnp.exp(g - m_new)
        decay = jnp.exp(bl + m_prev - m_new)
        c_sc[h] = decay * c_prev + _dot_tn((wk * vv[:, sl]).astype(BF16), kh)
        n_sc[h] = decay * n_prev + jnp.sum(wk * kc[:, sl], axis=0, keepdims=True)
        m_sc[h] = jnp.broadcast_to(m_new, m_sc.shape[1:])


def _mlstm(z, conv_w, gates_col, gates_row, *, L):
    S = z.shape[0]
    r = L // CONV_HALO
    W = MLSTM_HEADS * MLSTM_HEAD_DIM
    halo = lambda c: jnp.maximum(c * r - 1, 0)
    return pl.pallas_call(
        _mlstm_kernel,
        grid=(S // L,),
        in_specs=[
            pl.BlockSpec((L, W), lambda c: (c, COL_ML_Q // W)),
            pl.BlockSpec((L, W), lambda c: (c, COL_ML_K // W)),
            pl.BlockSpec((L, W), lambda c: (c, COL_ML_V // W)),
            pl.BlockSpec((L, W), lambda c: (c, COL_ML_O // W)),
            pl.BlockSpec((CONV_HALO, W), lambda c: (halo(c), COL_ML_Q // W)),
            pl.BlockSpec((CONV_HALO, W), lambda c: (halo(c), COL_ML_K // W)),
            pl.BlockSpec((MLSTM_CONV, 2 * W), lambda c: (0, 0)),
            pl.BlockSpec((L, N_GATES), lambda c: (c, 0)),
            pl.BlockSpec((N_GATES, L), lambda c: (0, c)),
        ],
        out_specs=pl.BlockSpec((L, W), lambda c: (c, 0)),
        out_shape=jax.ShapeDtypeStruct((S, W), BF16),
        scratch_shapes=[pltpu.VMEM((MLSTM_HEADS, MLSTM_HEAD_DIM, MLSTM_HEAD_DIM), F32),
                        pltpu.VMEM((MLSTM_HEADS, 1, MLSTM_HEAD_DIM), F32),
                        pltpu.VMEM((MLSTM_HEADS, 8, LANES), F32)],
        compiler_params=_cparams(("arbitrary",)),
        name="mlstm",
    )(z, z, z, z, z, z, conv_w, gates_col, gates_row)


def _fox_kernel(q_ref, k_ref, v_ref, cq_ref, ck_ref, o_ref, m_sc, acc_sc, s_sc, *, t):
    i = pl.program_id(1)
    nch = t // LANES
    m_sc[...] = jnp.full_like(m_sc, -jnp.inf)
    acc_sc[...] = jnp.zeros_like(acc_sc)

    q = q_ref[...] * (LOG2E * HEAD_DIM ** -0.5)
    lane = lax.broadcasted_iota(jnp.int32, (t, LANES), 1)
    row = lax.broadcasted_iota(jnp.int32, (t, LANES), 0)
    low = lane < HEAD_DIM
    own = [low, ~low]
    qa = [jnp.where(own[a], q, 0.0).astype(BF16) for a in range(2)]
    col0 = FOX_COL0 + 2 * pl.program_id(0)
    gate_lane = lax.broadcasted_iota(jnp.int32, (t, N_GATES), 1)
    gate_row = lax.broadcasted_iota(jnp.int32, (N_GATES, t), 0)
    gq = cq_ref[...] * LOG2E
    ca = [jnp.broadcast_to(jnp.sum(jnp.where(gate_lane == col0 + a, gq, 0.0), axis=1, keepdims=True),
                           (t, LANES)) for a in range(2)]

    def scores(j, slot):
        j0 = pl.multiple_of(j * t, t)
        kb = k_ref[pl.ds(j0, t), :]
        for a in range(2):
            s_sc[slot, a] = _dot_nt(qa[a], kb)

    def softmax_pv(j, slot, masked):
        j0 = pl.multiple_of(j * t, t)
        vb = v_ref[pl.ds(j0, t), :]
        gk = ck_ref[:, pl.ds(j0, t)] * LOG2E
        ck = jnp.concatenate([jnp.sum(jnp.where(gate_row == col0 + a, gk, 0.0), axis=0, keepdims=True)
                              for a in range(2)], axis=0)
        ps, alphas = [], []
        for a in range(2):
            sc = []
            for c in range(nch):
                x = s_sc[slot, a, :, c * LANES:(c + 1) * LANES] - ck[a:a + 1, c * LANES:(c + 1) * LANES]
                if masked:
                    x = jnp.where(lane + c * LANES <= row, x, -jnp.inf)
                sc.append(x)
            mx = sc[0]
            for c in range(1, nch):
                mx = jnp.maximum(mx, sc[c])
            m_old = m_sc[a]
            m_new = jnp.maximum(m_old, jnp.max(mx, axis=1, keepdims=True) + ca[a])
            alpha = jnp.exp2(m_old - m_new)
            mm = m_new - ca[a]
            ps += [jnp.exp2(x - mm).astype(BF16) for x in sc]
            alphas.append(alpha)
            m_sc[a] = m_new
        one, zero = jnp.ones_like(vb), jnp.zeros_like(vb)
        vblk = jnp.concatenate([jnp.concatenate([jnp.where(low, vb, one), zero], axis=1),
                                jnp.concatenate([zero, jnp.where(low, one, vb)], axis=1)], axis=0)
        acc_sc[...] = jnp.concatenate(alphas, axis=1) * acc_sc[...] + _dot(jnp.concatenate(ps, axis=1), vblk)

    scores(0, 0)

    def pair(j):
        scores(j + 1, 1)
        softmax_pv(j, 0, False)
        scores(j + 2, 0)
        softmax_pv(j + 1, 1, False)

    def body(j4, carry):
        pair(4 * j4)
        pair(4 * j4 + 2)
        return carry

    lax.fori_loop(0, i // 4, body, 0)

    @pl.when(i % 4 >= 2)
    def _():
        pair(4 * (i // 4))

    @pl.when(i % 2 == 1)
    def _():
        scores(i, 1)
        softmax_pv(i - 1, 0, False)
        softmax_pv(i, 1, True)

    @pl.when(i % 2 == 0)
    def _():
        softmax_pv(i, 0, True)

    outs = []
    for a in range(2):
        acc = acc_sc[:, a * LANES:(a + 1) * LANES]
        outs.append(acc / pltpu.roll(acc, HEAD_DIM, axis=1))
    o_ref[...] = jnp.where(low, outs[0], outs[1]).astype(BF16)


def _fox(z, zb, gates_col, gates_row, *, t):
    S = z.shape[0]
    npair = FOX_HEADS // 2
    cq0, ck0, cv0 = COL_FOX_Q // LANES, COL_FOX_K // LANES, COL_FOX_V // LANES
    return pl.pallas_call(
        functools.partial(_fox_kernel, t=t),
        grid=(npair, S // t),
        in_specs=[
            pl.BlockSpec((t, LANES), lambda p, i: (i, cq0 + p)),
            pl.BlockSpec((S, LANES), lambda p, i: (0, ck0 + p)),
            pl.BlockSpec((S, LANES), lambda p, i: (0, cv0 + p)),
            pl.BlockSpec((t, N_GATES), lambda p, i: (i, 0)),
            pl.BlockSpec((N_GATES, S), lambda p, i: (0, 0)),
        ],
        out_specs=pl.BlockSpec((t, LANES), lambda p, i: (i, p)),
        out_shape=jax.ShapeDtypeStruct((S, MIX_WIDTH), BF16),
        scratch_shapes=[pltpu.VMEM((2, t, LANES), F32), pltpu.VMEM((t, 2 * LANES), F32),
                        pltpu.VMEM((2, 2, t, t), F32)],
        compiler_params=_cparams(("parallel", "arbitrary")),
        name="fox",
    )(z, zb, zb, gates_col, gates_row)


def _merge_kernel(hb_ref, ya_ref, yb_ref, yc_ref, yd_ref, wg_ref, wb_ref, o_ref):
    hb = hb_ref[...]
    acc = None
    for b, y_ref in enumerate((ya_ref, yb_ref, yc_ref, yd_ref)):
        gate = _sigmoid(_dot(hb, wg_ref[b]))
        term = gate * _dot(y_ref[...], wb_ref[b])
        acc = term if acc is None else acc + term
    o_ref[...] = acc.astype(BF16)


def _merge(hb, ys, wg, wb, layer, *, tm, tn):
    S, D = hb.shape
    _, nb, W, _ = wb.shape
    yspec = pl.BlockSpec((tm, W), lambda n, i: (i, 0))
    return pl.pallas_call(
        _merge_kernel,
        grid=(D // tn, S // tm),
        in_specs=[pl.BlockSpec((tm, D), lambda n, i: (i, 0)), yspec, yspec, yspec, yspec,
                  pl.BlockSpec((None, nb, D, tn), lambda n, i: (layer, 0, 0, n)),
                  pl.BlockSpec((None, nb, W, tn), lambda n, i: (layer, 0, 0, n))],
        out_specs=pl.BlockSpec((tm, tn), lambda n, i: (i, n)),
        out_shape=jax.ShapeDtypeStruct((S, D), BF16),
        compiler_params=_cparams(("parallel", "arbitrary")),
        name="merge",
    )(hb, *ys, wg, wb)


def _outproj_kernel(m_ref, h_ref, w_ref, g_ref, b_ref, o_ref, *, alpha):
    g, b = g_ref[...], b_ref[...]
    rows = 128
    for r in range(m_ref.shape[0] // rows):
        rs = slice(r * rows, (r + 1) * rows)
        y = alpha * h_ref[rs, :] + _dot(m_ref[rs, :], w_ref[...])
        o_ref[rs, :] = _layer_norm(y, g, b)


def _outproj(merged, h, w, layer, g, b, *, alpha, tm):
    S, D = h.shape
    return pl.pallas_call(
        functools.partial(_outproj_kernel, alpha=alpha),
        grid=(S // tm,),
        in_specs=[pl.BlockSpec((tm, D), lambda i: (i, 0)), pl.BlockSpec((tm, D), lambda i: (i, 0)),
                  pl.BlockSpec((None, D, D), lambda i: (layer, 0, 0)),
                  pl.BlockSpec((1, D), lambda i: (0, 0)), pl.BlockSpec((1, D), lambda i: (0, 0))],
        out_specs=pl.BlockSpec((tm, D), lambda i: (i, 0)),
        out_shape=jax.ShapeDtypeStruct((S, D), F32),
        compiler_params=_cparams(("parallel",)),
        name="outproj",
    )(merged, h, w, g.reshape(1, D), b.reshape(1, D))


def _rope_tables(S):
    half = HEAD_DIM // 2
    inv = ROPE_THETA ** (-jnp.arange(half, dtype=F32) / half)
    blk = SWA_BLOCK
    base = (jnp.arange(S // blk, dtype=jnp.int32) * blk).astype(F32)[:, None] * inv[None, :]
    off = jnp.arange(blk, dtype=jnp.int32).astype(F32)[:, None] * inv[None, :]
    cb, sb = jnp.cos(base)[:, None, :], jnp.sin(base)[:, None, :]
    co, so = jnp.cos(off)[None], jnp.sin(off)[None]
    cos = (cb * co - sb * so).reshape(S, half)
    sin = (sb * co + cb * so).reshape(S, half)
    cos_t = jnp.concatenate([cos, cos, cos, cos], axis=1)
    sin_t = jnp.concatenate([-sin, -sin, sin, sin], axis=1)
    return cos_t, sin_t


def _pick(n, prefs):
    for p in prefs:
        if n % p == 0:
            return p
    return n


def kernel(x, ln_g, ln_b, ffn1_w_in, ffn1_w_out, w_in, pool_w, pool_scale, swa_sinks, mlstm_conv, mlstm_i_bias, mlstm_f_bias, fox_f_bias, w_branch, w_gate, w_out, ffn2_w_in, ffn2_w_out):
    B, S, D = x.shape
    depth = ln_g.shape[0]
    assert B == 1 and w_in.shape[2] == 4880 and S % ML_CHUNK == 0
    alpha = float((2 * depth) ** 0.25)
    cos_t, sin_t = _rope_tables(S)
    tm = _pick(S, (512, 256))
    tm_ffn = _pick(S, (1024, 512, 256))
    t_mix = _pick(S, (512, 256))
    tq = _pick(S, (512, 256))

    g = SWA_HEADS // SWA_KV_HEADS
    wi = w_in.astype(BF16)
    hh = HEAD_DIM // 2
    swa_q = (wi[..., 512:1024].reshape(depth, D, SWA_KV_HEADS, g, 2, hh)
             .transpose(0, 1, 3, 4, 2, 5).reshape(depth, D, MIX_WIDTH))
    swa_k = (wi[..., 1024:1152].reshape(depth, D, SWA_KV_HEADS, 2, hh)
             .swapaxes(2, 3).reshape(depth, D, SWA_KV_HEADS * HEAD_DIM))
    wz = jnp.concatenate([
        wi[..., 0:512], swa_q, wi[..., 1280:3328], wi[..., 3336:4872],
        swa_k, wi[..., 1152:1280], wi[..., 3328:3336], wi[..., 4872:4880],
        jnp.zeros((depth, D, Z_COLS - COL_GATES - N_GATES), BF16)], axis=2)
    wbr = w_branch.astype(BF16)
    swa_rows = wbr[:, 1].reshape(depth, SWA_KV_HEADS, g, HEAD_DIM, D).swapaxes(1, 2).reshape(depth, 1, MIX_WIDTH, D)
    wb = jnp.concatenate([wbr[:, :1], swa_rows, wbr[:, 2:]], axis=1)
    wg = w_gate.astype(BF16)
    wo = w_out.astype(BF16)
    f1_in, f1_out = ffn1_w_in.astype(BF16), ffn1_w_out.astype(BF16)
    f2_in, f2_out = ffn2_w_in.astype(BF16), ffn2_w_out.astype(BF16)

    h = x.reshape(S, D)
    for l in range(depth):
        gate_bias = jnp.concatenate([mlstm_i_bias[l], mlstm_f_bias[l], fox_f_bias[l]]).reshape(1, N_GATES)
        sinks = swa_sinks[l]

        h, hb = _ffn(h, f1_in, f1_out, l, ln_g[l, 0], ln_b[l, 0], alpha=alpha, tm=tm_ffn, tf=512)
        z, zb = _inproj(hb, wz, l, tm=tm, tn=Z_COLS // 2)
        gates_col = _gates(z, gate_bias, T=ML_CHUNK)
        gates_row = gates_col.T

        ya = _pool(z, pool_w[l].astype(BF16), pool_scale[l], T=t_mix)
        yb = _swa(z, sinks, cos_t, sin_t, T=t_mix)
        yc = _mlstm(z, mlstm_conv[l], gates_col, gates_row, L=ML_CHUNK)
        yd = _fox(z, zb, gates_col, gates_row, t=tq)
        merged = _merge(hb, (ya, yb, yc, yd), wg, wb, l, tm=_pick(S, (1024, 512, 256)), tn=512)
        h = _outproj(merged, h, wo, l, ln_g[l, 1], ln_b[l, 1], alpha=alpha, tm=tm)
        h, hb = _ffn(h, f2_in, f2_out, l, ln_g[l, 2], ln_b[l, 2], alpha=alpha, tm=tm_ffn, tf=512)
    return h.reshape(B, S, D)
```

```python
import functools

import jax
import jax.numpy as jnp
from jax import lax
from jax.experimental import pallas as pl
from jax.experimental.pallas import tpu as pltpu

F32 = jnp.float32
BF16 = jnp.bfloat16

MIX_WIDTH = 512
HEAD_DIM = 64
POOL_WINDOWS = (2, 4, 8, 16)
POOL_CH = 128
SWA_HEADS = 8
SWA_KV_HEADS = 2
SWA_BLOCK = 128
ROPE_THETA = 10000.0
MLSTM_HEADS = 4
MLSTM_HEAD_DIM = 128
MLSTM_CONV = 4
FOX_HEADS = 8
LN_EPS = 1e-5
FFN_HALF = 0.5
LOG2E = 1.4426950408889634

LANES = 128
VMEM_LIMIT = 60 * 1024 * 1024

COL_POOL = 0
COL_SWA_Q = 512
COL_ML_Q = 1024
COL_ML_K = 1536
COL_ML_V = 2048
COL_ML_O = 2560
COL_FOX_Q = 3072
COL_FOX_K = 3584
COL_FOX_V = 4096
COL_SWA_K = 4608
COL_SWA_V = 4736
COL_GATES = 4864
Z_COLS = 5120
N_GATES = 16
FOX_COL0 = 2 * MLSTM_HEADS

ML_CHUNK = 512


def _cparams(sem):
    return pltpu.CompilerParams(dimension_semantics=sem, vmem_limit_bytes=VMEM_LIMIT)


def _layer_norm(y, g, b):
    mu = jnp.mean(y, axis=-1, keepdims=True)
    d = y - mu
    var = jnp.mean(d * d, axis=-1, keepdims=True)
    return d * lax.rsqrt(var + LN_EPS) * g + b


def _sigmoid(x):
    return 0.5 * jnp.tanh(0.5 * x) + 0.5


def _silu(x):
    h = 0.5 * x
    return h * jnp.tanh(h) + h


def _dot(a, b):
    return jnp.dot(a, b, preferred_element_type=F32)


def _dot_nt(a, b):
    return lax.dot_general(a, b, (((1,), (1,)), ((), ())), preferred_element_type=F32)


def _dot_tn(a, b):
    return lax.dot_general(a, b, (((0,), (0,)), ((), ())), preferred_element_type=F32)


def _ffn_kernel(x_ref, wg_ref, wu_ref, wo_ref, g_ref, b_ref, o_ref, ob_ref, acc_ref, *, alpha, nf):
    f = pl.program_id(1)
    tm = x_ref.shape[0]
    half = tm // 2
    rows = 64

    def layer_norm_rows(r0, n):
        g, b = g_ref[...], b_ref[...]
        for c in range(n // rows):
            rs = slice(r0 + c * rows, r0 + (c + 1) * rows)
            y = alpha * x_ref[rs, :] + FFN_HALF * acc_ref[rs, :]
            o = _layer_norm(y, g, b)
            o_ref[rs, :] = o
            ob_ref[rs, :] = o.astype(BF16)

    def step(first, last):
        for r in range(2):
            rs = slice(r * half, (r + 1) * half)
            xb = x_ref[rs, :].astype(BF16)
            gate = _dot(xb, wg_ref[...])
            up = _dot(xb, wu_ref[...])
            act = (_silu(gate) * up).astype(BF16)
            nw = acc_ref.shape[1] // 2
            for n in range(2):
                cols = slice(n * nw, (n + 1) * nw)
                d = _dot(act, wo_ref[:, cols])
                if first:
                    acc_ref[rs, cols] = d
                else:
                    acc_ref[rs, cols] += d
            if last:
                layer_norm_rows(r * half, half)

    if nf == 1:
        step(True, True)
    else:
        pl.when(f == 0)(lambda: step(True, False))
        pl.when((f > 0) & (f < nf - 1))(lambda: step(False, False))
        pl.when(f == nf - 1)(lambda: step(False, True))


def _ffn(x, w_in, w_out, layer, g, b, *, alpha, tm, tf):
    S, D = x.shape
    DF = w_out.shape[1]
    nf = DF // tf
    assert S % tm == 0 and DF % tf == 0
    return pl.pallas_call(
        functools.partial(_ffn_kernel, alpha=alpha, nf=nf),
        grid=(S // tm, nf),
        in_specs=[
            pl.BlockSpec((tm, D), lambda i, f: (i, 0)),
            pl.BlockSpec((None, D, tf), lambda i, f: (layer, 0, f)),
            pl.BlockSpec((None, D, tf), lambda i, f: (layer, 0, f + nf)),
            pl.BlockSpec((None, tf, D), lambda i, f: (layer, f, 0)),
            pl.BlockSpec((1, D), lambda i, f: (0, 0)),
            pl.BlockSpec((1, D), lambda i, f: (0, 0)),
        ],
        out_specs=[pl.BlockSpec((tm, D), lambda i, f: (i, 0), pipeline_mode=pl.Buffered(1)),
                   pl.BlockSpec((tm, D), lambda i, f: (i, 0), pipeline_mode=pl.Buffered(1))],
        out_shape=[jax.ShapeDtypeStruct((S, D), F32), jax.ShapeDtypeStruct((S, D), BF16)],
        scratch_shapes=[pltpu.VMEM((tm, D), F32)],
        compiler_params=_cparams(("parallel", "arbitrary")),
        name="ffn",
    )(x, w_in, w_in, w_out, g.reshape(1, D), b.reshape(1, D))


def _inproj_kernel(x_ref, w_ref, o_ref, ob_ref):
    z = _dot(x_ref[...], w_ref[...])
    o_ref[...] = z
    ob_ref[...] = z.astype(BF16)


def _inproj(xb, w, layer, *, tm, tn):
    S, D = xb.shape
    N = w.shape[2]
    assert S % tm == 0 and N % tn == 0
    ospec = pl.BlockSpec((tm, tn), lambda j, i: (i, j))
    return pl.pallas_call(
        _inproj_kernel,
        grid=(N // tn, S // tm),
        in_specs=[pl.BlockSpec((tm, D), lambda j, i: (i, 0)),
                  pl.BlockSpec((None, D, tn), lambda j, i: (layer, 0, j))],
        out_specs=[ospec, ospec],
        out_shape=[jax.ShapeDtypeStruct((S, N), F32), jax.ShapeDtypeStruct((S, N), BF16)],
        compiler_params=_cparams(("parallel", "arbitrary")),
        name="inproj",
    )(xb, w)


def _split3(x):
    hi = x.astype(BF16)
    r1 = x - hi.astype(F32)
    mid = r1.astype(BF16)
    lo = (r1 - mid.astype(F32)).astype(BF16)
    return hi, mid, lo


def _gates_kernel(z_ref, bias_ref, o_ref, carry_ref):
    c = pl.program_id(0)
    T = z_ref.shape[0]

    @pl.when(c == 0)
    def _():
        carry_ref[...] = jnp.zeros_like(carry_ref)

    pre = z_ref[:, 0:N_GATES] + bias_ref[...]
    lf = jax.nn.log_sigmoid(pre)
    row = lax.broadcasted_iota(jnp.int32, (T, T), 0)
    col = lax.broadcasted_iota(jnp.int32, (T, T), 1)
    tri = jnp.where(col <= row, 1.0, 0.0).astype(BF16)
    hi, mid, lo = _split3(lf)
    cs = _dot(tri, hi) + _dot(tri, mid) + _dot(tri, lo)
    lane = lax.broadcasted_iota(jnp.int32, (T, N_GATES), 1)
    is_fox = lane >= 2 * MLSTM_HEADS
    cs = cs + jnp.where(is_fox, carry_ref[...], 0.0)
    carry_ref[...] = cs[T - 1:T, :]
    o_ref[...] = jnp.where(lane < MLSTM_HEADS, pre, cs)


def _gates(z, bias, *, T):
    S = z.shape[0]
    return pl.pallas_call(
        _gates_kernel,
        grid=(S // T,),
        in_specs=[pl.BlockSpec((T, LANES), lambda c: (c, COL_GATES // LANES)),
                  pl.BlockSpec((1, N_GATES), lambda c: (0, 0))],
        out_specs=pl.BlockSpec((T, N_GATES), lambda c: (c, 0)),
        out_shape=jax.ShapeDtypeStruct((S, N_GATES), F32),
        scratch_shapes=[pltpu.VMEM((1, N_GATES), F32)],
        compiler_params=_cparams(("arbitrary",)),
        name="gates",
    )(z, bias)


POOL_HALO = 16


def _pool_kernel(u_ref, h_ref, w_ref, s_ref, o_ref):
    i = pl.program_id(0)
    T = u_ref.shape[0]
    u = u_ref[...]
    halo = jnp.where(i > 0, h_ref[...], 0.0)
    full = jnp.concatenate([halo, u], axis=0)
    t = i * T + lax.broadcasted_iota(jnp.int32, (T, 1), 0) + 1
    outs = []
    for g, win in enumerate(POOL_WINDOWS):
        cols = slice(g * POOL_CH, (g + 1) * POOL_CH)
        fg = full[:, cols]
        acc = fg[POOL_HALO:POOL_HALO + T]
        for j in range(1, win):
            acc = acc + fg[POOL_HALO - j:POOL_HALO - j + T]
        count = jnp.minimum(t, win).astype(F32)
        pooled = acc / count - u[:, cols]
        outs.append(_dot(pooled.astype(BF16), w_ref[g]))
    y = jnp.concatenate(outs, axis=1) * s_ref[...]
    o_ref[...] = y.astype(BF16)


def _pool(z, w, scale, *, T):
    S = z.shape[0]
    r = T // POOL_HALO
    return pl.pallas_call(
        _pool_kernel,
        grid=(S // T,),
        in_specs=[
            pl.BlockSpec((T, MIX_WIDTH), lambda i: (i, 0)),
            pl.BlockSpec((POOL_HALO, MIX_WIDTH), lambda i: (jnp.maximum(i * r - 1, 0), 0)),
            pl.BlockSpec((len(POOL_WINDOWS), POOL_CH, POOL_CH), lambda i: (0, 0, 0)),
            pl.BlockSpec((1, MIX_WIDTH), lambda i: (0, 0)),
        ],
        out_specs=pl.BlockSpec((T, MIX_WIDTH), lambda i: (i, 0)),
        out_shape=jax.ShapeDtypeStruct((S, MIX_WIDTH), BF16),
        compiler_params=_cparams(("parallel",)),
        name="pool",
    )(z, z, w, scale.reshape(1, MIX_WIDTH))


def _rope(x, cos, sin_signed):
    n = x.shape[1] // LANES
    outs = []
    for s in range(n):
        xs = x[:, s * LANES:(s + 1) * LANES]
        outs.append(xs * cos + pltpu.roll(xs, LANES // 2, axis=1) * sin_signed)
    return outs[0] if n == 1 else jnp.concatenate(outs, axis=1)


def _swa_kernel(sink_ref, q_ref, k_ref, v_ref, kh_ref, vh_ref, cos_ref, sin_ref, cosh_ref, sinh_ref, o_ref, *, nblk):
    i = pl.program_id(0)
    B = SWA_BLOCK
    G = SWA_HEADS // SWA_KV_HEADS
    cos = cos_ref[...]
    sin = sin_ref[...]
    q = (_rope(q_ref[...], cos, sin) * (HEAD_DIM ** -0.5)).astype(BF16)
    k = jnp.concatenate([_rope(kh_ref[...], cosh_ref[...], sinh_ref[...]), _rope(k_ref[...], cos, sin)],
                        axis=0).astype(BF16)
    v = jnp.concatenate([vh_ref[...], v_ref[...]], axis=0).astype(BF16)

    lane = lax.broadcasted_iota(jnp.int32, (G * B, LANES), 1)
    low = lane < HEAD_DIM
    low_qk = (lane % HEAD_DIM) < (HEAD_DIM // 2)
    low_v = lax.broadcasted_iota(jnp.int32, (2 * B, LANES), 1) < HEAD_DIM
    rowq = lax.broadcasted_iota(jnp.int32, (G * B, 2 * B), 0) % B
    colk = lax.broadcasted_iota(jnp.int32, (G * B, 2 * B), 1)
    band = (colk > rowq) & (colk <= rowq + B)
    grp = lax.broadcasted_iota(jnp.int32, (G * B, LANES), 0) // B
    sinks = []
    for kvh in range(SWA_KV_HEADS):
        col = jnp.zeros((G * B, LANES), F32)
        for g in range(G):
            col = jnp.where(grp == g, sink_ref[kvh * G + g], col)
        sinks.append(col)

    for blk in range(nblk):
        r0 = blk * B
        qb = jnp.concatenate([q[r0:r0 + B, s * LANES:(s + 1) * LANES] for s in range(G)], axis=0)
        kb = k[r0:r0 + 2 * B]
        vb = v[r0:r0 + 2 * B]
        valid = band
        if blk == 0:
            valid = band & ((colk >= B) | (i > 0))
        pvs = []
        for kvh in range(SWA_KV_HEADS):
            qm = jnp.where(low_qk if kvh == 0 else ~low_qk, qb, jnp.zeros_like(qb))
            s = _dot_nt(qm, kb)
            s = jnp.where(valid, s, -jnp.inf)
            sc = [s[:, c * LANES:(c + 1) * LANES] for c in range(2 * B // LANES)]
            mx = sc[0]
            for x in sc[1:]:
                mx = jnp.maximum(mx, x)
            m = jnp.maximum(jnp.max(mx, axis=1, keepdims=True), sinks[kvh])
            p = jnp.concatenate([jnp.exp(x - m).astype(BF16) for x in sc], axis=1)
            pv = _dot(p, jnp.where(low_v if kvh == 0 else ~low_v, vb, jnp.ones_like(vb)))
            den = pltpu.roll(pv, HEAD_DIM, axis=1) + jnp.exp(sinks[kvh] - m)
            pvs.append(pv / den)
        o = jnp.where(low, pvs[0], pvs[1])
        for s_ in range(G):
            o_ref[r0:r0 + B, s_ * LANES:(s_ + 1) * LANES] = o[s_ * B:(s_ + 1) * B].astype(BF16)


def _swa(z, sinks, cos, sin, *, T):
    S = z.shape[0]
    nblk = T // SWA_BLOCK
    cq, ck, cv = COL_SWA_Q // MIX_WIDTH, COL_SWA_K // LANES, COL_SWA_V // LANES
    halo = lambda i: jnp.maximum(i * nblk - 1, 0)
    return pl.pallas_call(
        functools.partial(_swa_kernel, nblk=nblk),
        grid=(S // T,),
        in_specs=[
            pl.BlockSpec(memory_space=pltpu.SMEM),
            pl.BlockSpec((T, MIX_WIDTH), lambda i: (i, cq)),
            pl.BlockSpec((T, LANES), lambda i: (i, ck)),
            pl.BlockSpec((T, LANES), lambda i: (i, cv)),
            pl.BlockSpec((SWA_BLOCK, LANES), lambda i: (halo(i), ck)),
            pl.BlockSpec((SWA_BLOCK, LANES), lambda i: (halo(i), cv)),
            pl.BlockSpec((T, LANES), lambda i: (i, 0)),
            pl.BlockSpec((T, LANES), lambda i: (i, 0)),
            pl.BlockSpec((SWA_BLOCK, LANES), lambda i: (halo(i), 0)),
            pl.BlockSpec((SWA_BLOCK, LANES), lambda i: (halo(i), 0)),
        ],
        out_specs=pl.BlockSpec((T, MIX_WIDTH), lambda i: (i, 0)),
        out_shape=jax.ShapeDtypeStruct((S, MIX_WIDTH), BF16),
        compiler_params=_cparams(("parallel",)),
        name="swa",
    )(sinks, z, z, z, z, z, cos, sin, cos, sin)


CONV_HALO = 8


def _mlstm_kernel(q_ref, k_ref, v_ref, og_ref, qh_ref, kh_ref, cw_ref, gc_ref, gr_ref, o_ref, c_sc, n_sc, m_sc):
    c = pl.program_id(0)
    L = q_ref.shape[0]
    H, dh = MLSTM_HEADS, MLSTM_HEAD_DIM

    @pl.when(c == 0)
    def _():
        c_sc[...] = jnp.zeros_like(c_sc)
        n_sc[...] = jnp.zeros_like(n_sc)
        m_sc[...] = jnp.zeros_like(m_sc)

    def conv_silu(x_ref, halo_ref, w):
        halo = jnp.where(c > 0, halo_ref[...], 0.0)
        full = jnp.concatenate([halo, x_ref[...]], axis=0)
        acc = jnp.zeros((L, x_ref.shape[1]), F32)
        for j in range(MLSTM_CONV):
            off = CONV_HALO - (MLSTM_CONV - 1) + j
            acc = acc + w[j:j + 1, :] * full[off:off + L]
        return _silu(acc)

    cw = cw_ref[...]
    qc = conv_silu(q_ref, qh_ref, cw[:, :H * dh]) * (dh ** -0.5)
    kc = conv_silu(k_ref, kh_ref, cw[:, H * dh:])
    vv = v_ref[...]
    gc = gc_ref[...]
    gr = gr_ref[...]
    row = lax.broadcasted_iota(jnp.int32, (L, L), 0)
    col = lax.broadcasted_iota(jnp.int32, (L, L), 1)
    causal = col <= row

    for h in range(H):
        sl = slice(h * dh, (h + 1) * dh)
        qf = qc[:, sl]
        qh, kh, vh = qf.astype(BF16), kc[:, sl].astype(BF16), vv[:, sl].astype(BF16)
        i_col = gc[:, h:h + 1]
        b_col = gc[:, H + h:H + h + 1]
        a_row = gr[h:h + 1, :] - gr[H + h:H + h + 1, :]
        m_prev = m_sc[h][0:1, 0:1]
        c_prev = c_sc[h]
        n_prev = n_sc[h]

        dlog = jnp.where(causal, b_col + a_row, -jnp.inf)
        inter = b_col + m_prev
        m_comb = jnp.maximum(inter, jnp.max(dlog, axis=1, keepdims=True))
        sm = _dot_nt(qh, kh) * jnp.exp(dlog - m_comb)
        w_inter = jnp.exp(inter - m_comb)
        num = _dot(sm.astype(BF16), vh) + w_inter * _dot_nt(qh, c_prev.astype(BF16))
        den = jnp.sum(sm, axis=1, keepdims=True) + w_inter * jnp.sum(qf * n_prev, axis=1, keepdims=True)
        hid = num / jnp.maximum(jnp.abs(den), jnp.exp(-m_comb))
        o_ref[:, sl] = (_sigmoid(og_ref[:, sl]) * hid).astype(BF16)

        bl = b_col[L - 1:L, :]
        g = bl - b_col + i_col
        m_new = jnp.maximum(bl + m_prev, jnp.max(g, axis=0, keepdims=True))
        wk = jnp.exp(g - m_new)
        decay = jnp.exp(bl + m_prev - m_new)
        c_sc[h] = decay * c_prev + _dot_tn((wk * vv[:, sl]).astype(BF16), kh)
        n_sc[h] = decay * n_prev + jnp.sum(wk * kc[:, sl], axis=0, keepdims=True)
        m_sc[h] = jnp.broadcast_to(m_new, m_sc.shape[1:])


def _mlstm(z, conv_w, gates_col, gates_row, *, L):
    S = z.shape[0]
    r = L // CONV_HALO
    W = MLSTM_HEADS * MLSTM_HEAD_DIM
    halo = lambda c: jnp.maximum(c * r - 1, 0)
    return pl.pallas_call(
        _mlstm_kernel,
        grid=(S // L,),
        in_specs=[
            pl.BlockSpec((L, W), lambda c: (c, COL_ML_Q // W)),
            pl.BlockSpec((L, W), lambda c: (c, COL_ML_K // W)),
            pl.BlockSpec((L, W), lambda c: (c, COL_ML_V // W)),
            pl.BlockSpec((L, W), lambda c: (c, COL_ML_O // W)),
            pl.BlockSpec((CONV_HALO, W), lambda c: (halo(c), COL_ML_Q // W)),
            pl.BlockSpec((CONV_HALO, W), lambda c: (halo(c), COL_ML_K // W)),
            pl.BlockSpec((MLSTM_CONV, 2 * W), lambda c: (0, 0)),
            pl.BlockSpec((L, N_GATES), lambda c: (c, 0)),
            pl.BlockSpec((N_GATES, L), lambda c: (0, c)),
        ],
        out_specs=pl.BlockSpec((L, W), lambda c: (c, 0)),
        out_shape=jax.ShapeDtypeStruct((S, W), BF16),
        scratch_shapes=[pltpu.VMEM((MLSTM_HEADS, MLSTM_HEAD_DIM, MLSTM_HEAD_DIM), F32),
                        pltpu.VMEM((MLSTM_HEADS, 1, MLSTM_HEAD_DIM), F32),
                        pltpu.VMEM((MLSTM_HEADS, 8, LANES), F32)],
        compiler_params=_cparams(("arbitrary",)),
        name="mlstm",
    )(z, z, z, z, z, z, conv_w, gates_col, gates_row)


def _fox_kernel(q_ref, k_ref, v_ref, cq_ref, ck_ref, o_ref, m_sc, acc_sc, s_sc, *, t):
    i = pl.program_id(1)
    nch = t // LANES
    m_sc[...] = jnp.full_like(m_sc, -jnp.inf)
    acc_sc[...] = jnp.zeros_like(acc_sc)

    q = q_ref[...] * (LOG2E * HEAD_DIM ** -0.5)
    lane = lax.broadcasted_iota(jnp.int32, (t, LANES), 1)
    row = lax.broadcasted_iota(jnp.int32, (t, LANES), 0)
    low = lane < HEAD_DIM
    own = [low, ~low]
    qa = [jnp.where(own[a], q, 0.0).astype(BF16) for a in range(2)]
    col0 = FOX_COL0 + 2 * pl.program_id(0)
    gate_lane = lax.broadcasted_iota(jnp.int32, (t, N_GATES), 1)
    gate_row = lax.broadcasted_iota(jnp.int32, (N_GATES, t), 0)
    gq = cq_ref[...] * LOG2E
    ca = [jnp.broadcast_to(jnp.sum(jnp.where(gate_lane == col0 + a, gq, 0.0), axis=1, keepdims=True),
                           (t, LANES)) for a in range(2)]

    def scores(j, slot):
        j0 = pl.multiple_of(j * t, t)
        kb = k_ref[pl.ds(j0, t), :]
        for a in range(2):
            s_sc[slot, a] = _dot_nt(qa[a], kb)

    def softmax_pv(j, slot, masked):
        j0 = pl.multiple_of(j * t, t)
        vb = v_ref[pl.ds(j0, t), :]
        gk = ck_ref[:, pl.ds(j0, t)] * LOG2E
        ck = jnp.concatenate([jnp.sum(jnp.where(gate_row == col0 + a, gk, 0.0), axis=0, keepdims=True)
                              for a in range(2)], axis=0)
        ps, alphas = [], []
        for a in range(2):
            sc = []
            for c in range(nch):
                x = s_sc[slot, a, :, c * LANES:(c + 1) * LANES] - ck[a:a + 1, c * LANES:(c + 1) * LANES]
                if masked:
                    x = jnp.where(lane + c * LANES <= row, x, -jnp.inf)
                sc.append(x)
            mx = sc[0]
            for c in range(1, nch):
                mx = jnp.maximum(mx, sc[c])
            m_old = m_sc[a]
            m_new = jnp.maximum(m_old, jnp.max(mx, axis=1, keepdims=True) + ca[a])
            alpha = jnp.exp2(m_old - m_new)
            mm = m_new - ca[a]
            ps += [jnp.exp2(x - mm).astype(BF16) for x in sc]
            alphas.append(alpha)
            m_sc[a] = m_new
        one, zero = jnp.ones_like(vb), jnp.zeros_like(vb)
        vblk = jnp.concatenate([jnp.concatenate([jnp.where(low, vb, one), zero], axis=1),
                                jnp.concatenate([zero, jnp.where(low, one, vb)], axis=1)], axis=0)
        acc_sc[...] = jnp.concatenate(alphas, axis=1) * acc_sc[...] + _dot(jnp.concatenate(ps, axis=1), vblk)

    scores(0, 0)

    def pair(j):
        scores(j + 1, 1)
        softmax_pv(j, 0, False)
        scores(j + 2, 0)
        softmax_pv(j + 1, 1, False)

    def body(j4, carry):
        pair(4 * j4)
        pair(4 * j4 + 2)
        return carry

    lax.fori_loop(0, i // 4, body, 0)

    @pl.when(i % 4 >= 2)
    def _():
        pair(4 * (i // 4))

    @pl.when(i % 2 == 1)
    def _():
        scores(i, 1)
        softmax_pv(i - 1, 0, False)
        softmax_pv(i, 1, True)

    @pl.when(i % 2 == 0)
    def _():
        softmax_pv(i, 0, True)

    outs = []
    for a in range(2):
        acc = acc_sc[:, a * LANES:(a + 1) * LANES]
        outs.append(acc / pltpu.roll(acc, HEAD_DIM, axis=1))
    o_ref[...] = jnp.where(low, outs[0], outs[1]).astype(BF16)


def _fox(z, zb, gates_col, gates_row, *, t):
    S = z.shape[0]
    npair = FOX_HEADS // 2
    cq0, ck0, cv0 = COL_FOX_Q // LANES, COL_FOX_K // LANES, COL_FOX_V // LANES
    return pl.pallas_call(
        functools.partial(_fox_kernel, t=t),
        grid=(npair, S // t),
        in_specs=[
            pl.BlockSpec((t, LANES), lambda p, i: (i, cq0 + p)),
            pl.BlockSpec((S, LANES), lambda p, i: (0, ck0 + p)),
            pl.BlockSpec((S, LANES), lambda p, i: (0, cv0 + p)),
            pl.BlockSpec((t, N_GATES), lambda p, i: (i, 0)),
            pl.BlockSpec((N_GATES, S), lambda p, i: (0, 0)),
        ],
        out_specs=pl.BlockSpec((t, LANES), lambda p, i: (i, p)),
        out_shape=jax.ShapeDtypeStruct((S, MIX_WIDTH), BF16),
        scratch_shapes=[pltpu.VMEM((2, t, LANES), F32), pltpu.VMEM((t, 2 * LANES), F32),
                        pltpu.VMEM((2, 2, t, t), F32)],
        compiler_params=_cparams(("parallel", "arbitrary")),
        name="fox",
    )(z, zb, zb, gates_col, gates_row)


def _merge_kernel(hb_ref, ya_ref, yb_ref, yc_ref, yd_ref, wg_ref, wb_ref, o_ref):
    hb = hb_ref[...]
    acc = None
    for b, y_ref in enumerate((ya_ref, yb_ref, yc_ref, yd_ref)):
        gate = _sigmoid(_dot(hb, wg_ref[b]))
        term = gate * _dot(y_ref[...], wb_ref[b])
        acc = term if acc is None else acc + term
    o_ref[...] = acc.astype(BF16)


def _merge(hb, ys, wg, wb, layer, *, tm, tn):
    S, D = hb.shape
    _, nb, W, _ = wb.shape
    yspec = pl.BlockSpec((tm, W), lambda n, i: (i, 0))
    return pl.pallas_call(
        _merge_kernel,
        grid=(D // tn, S // tm),
        in_specs=[pl.BlockSpec((tm, D), lambda n, i: (i, 0)), yspec, yspec, yspec, yspec,
                  pl.BlockSpec((None, nb, D, tn), lambda n, i: (layer, 0, 0, n)),
                  pl.BlockSpec((None, nb, W, tn), lambda n, i: (layer, 0, 0, n))],
        out_specs=pl.BlockSpec((tm, tn), lambda n, i: (i, n)),
        out_shape=jax.ShapeDtypeStruct((S, D), BF16),
        compiler_params=_cparams(("parallel", "arbitrary")),
        name="merge",
    )(hb, *ys, wg, wb)


def _outproj_kernel(m_ref, h_ref, w_ref, g_ref, b_ref, o_ref, *, alpha):
    g, b = g_ref[...], b_ref[...]
    rows = 128
    for r in range(m_ref.shape[0] // rows):
        rs = slice(r * rows, (r + 1) * rows)
        y = alpha * h_ref[rs, :] + _dot(m_ref[rs, :], w_ref[...])
        o_ref[rs, :] = _layer_norm(y, g, b)


def _outproj(merged, h, w, layer, g, b, *, alpha, tm):
    S, D = h.shape
    return pl.pallas_call(
        functools.partial(_outproj_kernel, alpha=alpha),
        grid=(S // tm,),
        in_specs=[pl.BlockSpec((tm, D), lambda i: (i, 0)), pl.BlockSpec((tm, D), lambda i: (i, 0)),
                  pl.BlockSpec((None, D, D), lambda i: (layer, 0, 0)),
                  pl.BlockSpec((1, D), lambda i: (0, 0)), pl.BlockSpec((1, D), lambda i: (0, 0))],
        out_specs=pl.BlockSpec((tm, D), lambda i: (i, 0)),
        out_shape=jax.ShapeDtypeStruct((S, D), F32),
        compiler_params=_cparams(("parallel",)),
        name="outproj",
    )(merged, h, w, g.reshape(1, D), b.reshape(1, D))


def _rope_tables(S):
    half = HEAD_DIM // 2
    inv = ROPE_THETA ** (-jnp.arange(half, dtype=F32) / half)
    blk = SWA_BLOCK
    base = (jnp.arange(S // blk, dtype=jnp.int32) * blk).astype(F32)[:, None] * inv[None, :]
    off = jnp.arange(blk, dtype=jnp.int32).astype(F32)[:, None] * inv[None, :]
    cb, sb = jnp.cos(base)[:, None, :], jnp.sin(base)[:, None, :]
    co, so = jnp.cos(off)[None], jnp.sin(off)[None]
    cos = (cb * co - sb * so).reshape(S, half)
    sin = (sb * co + cb * so).reshape(S, half)
    cos_t = jnp.concatenate([cos, cos, cos, cos], axis=1)
    sin_t = jnp.concatenate([-sin, -sin, sin, sin], axis=1)
    return cos_t, sin_t


def _pick(n, prefs):
    for p in prefs:
        if n % p == 0:
            return p
    return n


def kernel(x, ln_g, ln_b, ffn1_w_in, ffn1_w_out, w_in, pool_w, pool_scale, swa_sinks, mlstm_conv, mlstm_i_bias, mlstm_f_bias, fox_f_bias, w_branch, w_gate, w_out, ffn2_w_in, ffn2_w_out):
    B, S, D = x.shape
    depth = ln_g.shape[0]
    assert B == 1 and w_in.shape[2] == 4880 and S % ML_CHUNK == 0
    alpha = float((2 * depth) ** 0.25)
    cos_t, sin_t = _rope_tables(S)
    tm = _pick(S, (512, 256))
    tm_ffn = _pick(S, (1024, 512, 256))
    t_mix = _pick(S, (512, 256))
    tq = _pick(S, (512, 256))

    g = SWA_HEADS // SWA_KV_HEADS
    wi = w_in.astype(BF16)
    hh = HEAD_DIM // 2
    swa_q = (wi[..., 512:1024].reshape(depth, D, SWA_KV_HEADS, g, 2, hh)
             .transpose(0, 1, 3, 4, 2, 5).reshape(depth, D, MIX_WIDTH))
    swa_k = (wi[..., 1024:1152].reshape(depth, D, SWA_KV_HEADS, 2, hh)
             .swapaxes(2, 3).reshape(depth, D, SWA_KV_HEADS * HEAD_DIM))
    wz = jnp.concatenate([
        wi[..., 0:512], swa_q, wi[..., 1280:3328], wi[..., 3336:4872],
        swa_k, wi[..., 1152:1280], wi[..., 3328:3336], wi[..., 4872:4880],
        jnp.zeros((depth, D, Z_COLS - COL_GATES - N_GATES), BF16)], axis=2)
    wbr = w_branch.astype(BF16)
    swa_rows = wbr[:, 1].reshape(depth, SWA_KV_HEADS, g, HEAD_DIM, D).swapaxes(1, 2).reshape(depth, 1, MIX_WIDTH, D)
    wb = jnp.concatenate([wbr[:, :1], swa_rows, wbr[:, 2:]], axis=1)
    wg = w_gate.astype(BF16)
    wo = w_out.astype(BF16)
    f1_in, f1_out = ffn1_w_in.astype(BF16), ffn1_w_out.astype(BF16)
    f2_in, f2_out = ffn2_w_in.astype(BF16), ffn2_w_out.astype(BF16)

    h = x.reshape(S, D)
    for l in range(depth):
        gate_bias = jnp.concatenate([mlstm_i_bias[l], mlstm_f_bias[l], fox_f_bias[l]]).reshape(1, N_GATES)
        sinks = swa_sinks[l]

        h, hb = _ffn(h, f1_in, f1_out, l, ln_g[l, 0], ln_b[l, 0], alpha=alpha, tm=tm_ffn, tf=512)
        z, zb = _inproj(hb, wz, l, tm=tm, tn=Z_COLS // 2)
        gates_col = _gates(z, gate_bias, T=ML_CHUNK)
        gates_row = gates_col.T

        ya = _pool(z, pool_w[l].astype(BF16), pool_scale[l], T=t_mix)
        yb = _swa(z, sinks, cos_t, sin_t, T=t_mix)
        yc = _mlstm(z, mlstm_conv[l], gates_col, gates_row, L=ML_CHUNK)
        yd = _fox(z, zb, gates_col, gates_row, t=tq)
        merged = _merge(hb, (ya, yb, yc, yd), wg, wb, l, tm=_pick(S, (1024, 512, 256)), tn=512)
        h = _outproj(merged, h, wo, l, ln_g[l, 1], ln_b[l, 1], alpha=alpha, tm=tm)
        h, hb = _ffn(h, f2_in, f2_out, l, ln_g[l, 2], ln_b[l, 2], alpha=alpha, tm=tm_ffn, tf=512)
    return h.reshape(B, S, D)
```

```python
import functools

import jax
import jax.numpy as jnp
from jax import lax
from jax.experimental import pallas as pl
from jax.experimental.pallas import tpu as pltpu

F32 = jnp.float32
BF16 = jnp.bfloat16

MIX_WIDTH = 512
HEAD_DIM = 64
POOL_WINDOWS = (2, 4, 8, 16)
POOL_CH = 128
SWA_HEADS = 8
SWA_KV_HEADS = 2
SWA_BLOCK = 128
ROPE_THETA = 10000.0
MLSTM_HEADS = 4
MLSTM_HEAD_DIM = 128
MLSTM_CONV = 4
FOX_HEADS = 8
LN_EPS = 1e-5
FFN_HALF = 0.5
LOG2E = 1.4426950408889634

LANES = 128
VMEM_LIMIT = 60 * 1024 * 1024

COL_POOL = 0
COL_SWA_Q = 512
COL_ML_Q = 1024
COL_ML_K = 1536
COL_ML_V = 2048
COL_ML_O = 2560
COL_FOX_Q = 3072
COL_FOX_K = 3584
COL_FOX_V = 4096
COL_SWA_K = 4608
COL_SWA_V = 4736
COL_GATES = 4864
Z_COLS = 5120
N_GATES = 16
FOX_COL0 = 2 * MLSTM_HEADS

ML_CHUNK = 512


def _cparams(sem):
    return pltpu.CompilerParams(dimension_semantics=sem, vmem_limit_bytes=VMEM_LIMIT)


def _layer_norm(y, g, b):
    mu = jnp.mean(y, axis=-1, keepdims=True)
    d = y - mu
    var = jnp.mean(d * d, axis=-1, keepdims=True)
    return d * lax.rsqrt(var + LN_EPS) * g + b


def _sigmoid(x):
    return 0.5 * jnp.tanh(0.5 * x) + 0.5


def _silu(x):
    h = 0.5 * x
    return h * jnp.tanh(h) + h


def _dot(a, b):
    return jnp.dot(a, b, preferred_element_type=F32)


def _dot_nt(a, b):
    return lax.dot_general(a, b, (((1,), (1,)), ((), ())), preferred_element_type=F32)


def _dot_tn(a, b):
    return lax.dot_general(a, b, (((0,), (0,)), ((), ())), preferred_element_type=F32)


def _ffn_kernel(x_ref, wg_ref, wu_ref, wo_ref, g_ref, b_ref, o_ref, ob_ref, acc_ref, *, alpha, nf):
    f = pl.program_id(1)
    tm = x_ref.shape[0]
    half = tm // 2
    rows = 64

    def layer_norm_rows(r0, n):
        g, b = g_ref[...], b_ref[...]
        for c in range(n // rows):
            rs = slice(r0 + c * rows, r0 + (c + 1) * rows)
            y = alpha * x_ref[rs, :] + FFN_HALF * acc_ref[rs, :]
            o = _layer_norm(y, g, b)
            o_ref[rs, :] = o
            ob_ref[rs, :] = o.astype(BF16)

    def step(first, last):
        for r in range(2):
            rs = slice(r * half, (r + 1) * half)
            xb = x_ref[rs, :].astype(BF16)
            gate = _dot(xb, wg_ref[...])
            up = _dot(xb, wu_ref[...])
            act = (_silu(gate) * up).astype(BF16)
            nw = acc_ref.shape[1] // 2
            for n in range(2):
                cols = slice(n * nw, (n + 1) * nw)
                d = _dot(act, wo_ref[:, cols])
                if first:
                    acc_ref[rs, cols] = d
                else:
                    acc_ref[rs, cols] += d
            if last:
                layer_norm_rows(r * half, half)

    if nf == 1:
        step(True, True)
    else:
        pl.when(f == 0)(lambda: step(True, False))
        pl.when((f > 0) & (f < nf - 1))(lambda: step(False, False))
        pl.when(f == nf - 1)(lambda: step(False, True))


def _ffn(x, w_in, w_out, layer, g, b, *, alpha, tm, tf):
    S, D = x.shape
    DF = w_out.shape[1]
    nf = DF // tf
    assert S % tm == 0 and DF % tf == 0
    return pl.pallas_call(
        functools.partial(_ffn_kernel, alpha=alpha, nf=nf),
        grid=(S // tm, nf),
        in_specs=[
            pl.BlockSpec((tm, D), lambda i, f: (i, 0)),
            pl.BlockSpec((None, D, tf), lambda i, f: (layer, 0, f)),
            pl.BlockSpec((None, D, tf), lambda i, f: (layer, 0, f + nf)),
            pl.BlockSpec((None, tf, D), lambda i, f: (layer, f, 0)),
            pl.BlockSpec((1, D), lambda i, f: (0, 0)),
            pl.BlockSpec((1, D), lambda i, f: (0, 0)),
        ],
        out_specs=[pl.BlockSpec((tm, D), lambda i, f: (i, 0), pipeline_mode=pl.Buffered(1)),
                   pl.BlockSpec((tm, D), lambda i, f: (i, 0), pipeline_mode=pl.Buffered(1))],
        out_shape=[jax.ShapeDtypeStruct((S, D), F32), jax.ShapeDtypeStruct((S, D), BF16)],
        scratch_shapes=[pltpu.VMEM((tm, D), F32)],
        compiler_params=_cparams(("parallel", "arbitrary")),
        name="ffn",
    )(x, w_in, w_in, w_out, g.reshape(1, D), b.reshape(1, D))


def _inproj_kernel(x_ref, w_ref, o_ref, ob_ref):
    z = _dot(x_ref[...], w_ref[...])
    o_ref[...] = z
    ob_ref[...] = z.astype(BF16)


def _inproj(xb, w, layer, *, tm, tn):
    S, D = xb.shape
    N = w.shape[2]
    assert S % tm == 0 and N % tn == 0
    ospec = pl.BlockSpec((tm, tn), lambda j, i: (i, j))
    return pl.pallas_call(
        _inproj_kernel,
        grid=(N // tn, S // tm),
        in_specs=[pl.BlockSpec((tm, D), lambda j, i: (i, 0)),
                  pl.BlockSpec((None, D, tn), lambda j, i: (layer, 0, j))],
        out_specs=[ospec, ospec],
        out_shape=[jax.ShapeDtypeStruct((S, N), F32), jax.ShapeDtypeStruct((S, N), BF16)],
        compiler_params=_cparams(("parallel", "arbitrary")),
        name="inproj",
    )(xb, w)


def _split3(x):
    hi = x.astype(BF16)
    r1 = x - hi.astype(F32)
    mid = r1.astype(BF16)
    lo = (r1 - mid.astype(F32)).astype(BF16)
    return hi, mid, lo


def _gates_kernel(z_ref, bias_ref, o_ref, carry_ref):
    c = pl.program_id(0)
    T = z_ref.shape[0]

    @pl.when(c == 0)
    def _():
        carry_ref[...] = jnp.zeros_like(carry_ref)

    pre = z_ref[:, 0:N_GATES] + bias_ref[...]
    lf = jax.nn.log_sigmoid(pre)
    row = lax.broadcasted_iota(jnp.int32, (T, T), 0)
    col = lax.broadcasted_iota(jnp.int32, (T, T), 1)
    tri = jnp.where(col <= row, 1.0, 0.0).astype(BF16)
    hi, mid, lo = _split3(lf)
    cs = _dot(tri, hi) + _dot(tri, mid) + _dot(tri, lo)
    lane = lax.broadcasted_iota(jnp.int32, (T, N_GATES), 1)
    is_fox = lane >= 2 * MLSTM_HEADS
    cs = cs + jnp.where(is_fox, carry_ref[...], 0.0)
    carry_ref[...] = cs[T - 1:T, :]
    o_ref[...] = jnp.where(lane < MLSTM_HEADS, pre, cs)


def _gates(z, bias, *, T):
    S = z.shape[0]
    return pl.pallas_call(
        _gates_kernel,
        grid=(S // T,),
        in_specs=[pl.BlockSpec((T, LANES), lambda c: (c, COL_GATES // LANES)),
                  pl.BlockSpec((1, N_GATES), lambda c: (0, 0))],
        out_specs=pl.BlockSpec((T, N_GATES), lambda c: (c, 0)),
        out_shape=jax.ShapeDtypeStruct((S, N_GATES), F32),
        scratch_shapes=[pltpu.VMEM((1, N_GATES), F32)],
        compiler_params=_cparams(("arbitrary",)),
        name="gates",
    )(z, bias)


POOL_HALO = 16


def _pool_kernel(u_ref, h_ref, w_ref, s_ref, o_ref):
    i = pl.program_id(0)
    T = u_ref.shape[0]
    u = u_ref[...]
    halo = jnp.where(i > 0, h_ref[...], 0.0)
    full = jnp.concatenate([halo, u], axis=0)
    t = i * T + lax.broadcasted_iota(jnp.int32, (T, 1), 0) + 1
    outs = []
    for g, win in enumerate(POOL_WINDOWS):
        cols = slice(g * POOL_CH, (g + 1) * POOL_CH)
        acc = full[:, cols]
        shift = 1
        while shift < win:
            acc = acc + pltpu.roll(acc, shift, axis=0)
            shift *= 2
        acc = acc[POOL_HALO:POOL_HALO + T]
        count = jnp.minimum(t, win).astype(F32)
        pooled = acc / count - u[:, cols]
        outs.append(_dot(pooled.astype(BF16), w_ref[g]))
    y = jnp.concatenate(outs, axis=1) * s_ref[...]
    o_ref[...] = y.astype(BF16)


def _pool(z, w, scale, *, T):
    S = z.shape[0]
    r = T // POOL_HALO
    return pl.pallas_call(
        _pool_kernel,
        grid=(S // T,),
        in_specs=[
            pl.BlockSpec((T, MIX_WIDTH), lambda i: (i, 0)),
            pl.BlockSpec((POOL_HALO, MIX_WIDTH), lambda i: (jnp.maximum(i * r - 1, 0), 0)),
            pl.BlockSpec((len(POOL_WINDOWS), POOL_CH, POOL_CH), lambda i: (0, 0, 0)),
            pl.BlockSpec((1, MIX_WIDTH), lambda i: (0, 0)),
        ],
        out_specs=pl.BlockSpec((T, MIX_WIDTH), lambda i: (i, 0)),
        out_shape=jax.ShapeDtypeStruct((S, MIX_WIDTH), BF16),
        compiler_params=_cparams(("parallel",)),
        name="pool",
    )(z, z, w, scale.reshape(1, MIX_WIDTH))


def _rope(x, cos, sin_signed):
    n = x.shape[1] // LANES
    outs = []
    for s in range(n):
        xs = x[:, s * LANES:(s + 1) * LANES]
        outs.append(xs * cos + pltpu.roll(xs, LANES // 2, axis=1) * sin_signed)
    return outs[0] if n == 1 else jnp.concatenate(outs, axis=1)


def _swa_kernel(sink_ref, q_ref, k_ref, v_ref, kh_ref, vh_ref, cos_ref, sin_ref, cosh_ref, sinh_ref, o_ref, *, nblk):
    i = pl.program_id(0)
    B = SWA_BLOCK
    G = SWA_HEADS // SWA_KV_HEADS
    cos = cos_ref[...]
    sin = sin_ref[...]
    q = (_rope(q_ref[...], cos, sin) * (HEAD_DIM ** -0.5)).astype(BF16)
    k = jnp.concatenate([_rope(kh_ref[...], cosh_ref[...], sinh_ref[...]), _rope(k_ref[...], cos, sin)],
                        axis=0).astype(BF16)
    v = jnp.concatenate([vh_ref[...], v_ref[...]], axis=0).astype(BF16)

    lane = lax.broadcasted_iota(jnp.int32, (G * B, LANES), 1)
    low = lane < HEAD_DIM
    low_qk = (lane % HEAD_DIM) < (HEAD_DIM // 2)
    low_v = lax.broadcasted_iota(jnp.int32, (2 * B, LANES), 1) < HEAD_DIM
    rowq = lax.broadcasted_iota(jnp.int32, (G * B, 2 * B), 0) % B
    colk = lax.broadcasted_iota(jnp.int32, (G * B, 2 * B), 1)
    band = (colk > rowq) & (colk <= rowq + B)
    grp = lax.broadcasted_iota(jnp.int32, (G * B, LANES), 0) // B
    sinks = []
    for kvh in range(SWA_KV_HEADS):
        col = jnp.zeros((G * B, LANES), F32)
        for g in range(G):
            col = jnp.where(grp == g, sink_ref[kvh * G + g], col)
        sinks.append(col)

    for blk in range(nblk):
        r0 = blk * B
        qb = jnp.concatenate([q[r0:r0 + B, s * LANES:(s + 1) * LANES] for s in range(G)], axis=0)
        kb = k[r0:r0 + 2 * B]
        vb = v[r0:r0 + 2 * B]
        valid = band
        if blk == 0:
            valid = band & ((colk >= B) | (i > 0))
        pvs = []
        for kvh in range(SWA_KV_HEADS):
            qm = jnp.where(low_qk if kvh == 0 else ~low_qk, qb, jnp.zeros_like(qb))
            s = _dot_nt(qm, kb)
            s = jnp.where(valid, s, -jnp.inf)
            sc = [s[:, c * LANES:(c + 1) * LANES] for c in range(2 * B // LANES)]
            mx = sc[0]
            for x in sc[1:]:
                mx = jnp.maximum(mx, x)
            m = jnp.maximum(jnp.max(mx, axis=1, keepdims=True), sinks[kvh])
            p = jnp.concatenate([jnp.exp(x - m).astype(BF16) for x in sc], axis=1)
            pv = _dot(p, jnp.where(low_v if kvh == 0 else ~low_v, vb, jnp.ones_like(vb)))
            den = pltpu.roll(pv, HEAD_DIM, axis=1) + jnp.exp(sinks[kvh] - m)
            pvs.append(pv / den)
        o = jnp.where(low, pvs[0], pvs[1])
        for s_ in range(G):
            o_ref[r0:r0 + B, s_ * LANES:(s_ + 1) * LANES] = o[s_ * B:(s_ + 1) * B].astype(BF16)


def _swa(z, sinks, cos, sin, *, T):
    S = z.shape[0]
    nblk = T // SWA_BLOCK
    cq, ck, cv = COL_SWA_Q // MIX_WIDTH, COL_SWA_K // LANES, COL_SWA_V // LANES
    halo = lambda i: jnp.maximum(i * nblk - 1, 0)
    return pl.pallas_call(
        functools.partial(_swa_kernel, nblk=nblk),
        grid=(S // T,),
        in_specs=[
            pl.BlockSpec(memory_space=pltpu.SMEM),
            pl.BlockSpec((T, MIX_WIDTH), lambda i: (i, cq)),
            pl.BlockSpec((T, LANES), lambda i: (i, ck)),
            pl.BlockSpec((T, LANES), lambda i: (i, cv)),
            pl.BlockSpec((SWA_BLOCK, LANES), lambda i: (halo(i), ck)),
            pl.BlockSpec((SWA_BLOCK, LANES), lambda i: (halo(i), cv)),
            pl.BlockSpec((T, LANES), lambda i: (i, 0)),
            pl.BlockSpec((T, LANES), lambda i: (i, 0)),
            pl.BlockSpec((SWA_BLOCK, LANES), lambda i: (halo(i), 0)),
            pl.BlockSpec((SWA_BLOCK, LANES), lambda i: (halo(i), 0)),
        ],
        out_specs=pl.BlockSpec((T, MIX_WIDTH), lambda i: (i, 0)),
        out_shape=jax.ShapeDtypeStruct((S, MIX_WIDTH), BF16),
        compiler_params=_cparams(("parallel",)),
        name="swa",
    )(sinks, z, z, z, z, z, cos, sin, cos, sin)


CONV_HALO = 8


def _mlstm_kernel(q_ref, k_ref, v_ref, og_ref, qh_ref, kh_ref, cw_ref, gc_ref, gr_ref, o_ref, c_sc, n_sc, m_sc):
    c = pl.program_id(0)
    L = q_ref.shape[0]
    H, dh = MLSTM_HEADS, MLSTM_HEAD_DIM

    @pl.when(c == 0)
    def _():
        c_sc[...] = jnp.zeros_like(c_sc)
        n_sc[...] = jnp.zeros_like(n_sc)
        m_sc[...] = jnp.zeros_like(m_sc)

    def conv_silu(x_ref, halo_ref, w):
        halo = jnp.where(c > 0, halo_ref[...], 0.0)
        full = jnp.concatenate([halo, x_ref[...]], axis=0)
        acc = jnp.zeros((L, x_ref.shape[1]), F32)
        for j in range(MLSTM_CONV):
            off = CONV_HALO - (MLSTM_CONV - 1) + j
            acc = acc + w[j:j + 1, :] * full[off:off + L]
        return _silu(acc)

    cw = cw_ref[...]
    qc = conv_silu(q_ref, qh_ref, cw[:, :H * dh]) * (dh ** -0.5)
    kc = conv_silu(k_ref, kh_ref, cw[:, H * dh:])
    vv = v_ref[...]
    gc = gc_ref[...]
    gr = gr_ref[...]
    row = lax.broadcasted_iota(jnp.int32, (L, L), 0)
    col = lax.broadcasted_iota(jnp.int32, (L, L), 1)
    causal = col <= row

    for h in range(H):
        sl = slice(h * dh, (h + 1) * dh)
        qf = qc[:, sl]
        qh, kh, vh = qf.astype(BF16), kc[:, sl].astype(BF16), vv[:, sl].astype(BF16)
        i_col = gc[:, h:h + 1]
        b_col = gc[:, H + h:H + h + 1]
        a_row = gr[h:h + 1, :] - gr[H + h:H + h + 1, :]
        m_prev = m_sc[h][0:1, 0:1]
        c_prev = c_sc[h]
        n_prev = n_sc[h]

        dlog = jnp.where(causal, b_col + a_row, -jnp.inf)
        inter = b_col + m_prev
        m_comb = jnp.maximum(inter, jnp.max(dlog, axis=1, keepdims=True))
        sm = _dot_nt(qh, kh) * jnp.exp(dlog - m_comb)
        w_inter = jnp.exp(inter - m_comb)
        num = _dot(sm.astype(BF16), vh) + w_inter * _dot_nt(qh, c_prev.astype(BF16))
        den = jnp.sum(sm, axis=1, keepdims=True) + w_inter * jnp.sum(qf * n_prev, axis=1, keepdims=True)
        hid = num / jnp.maximum(jnp.abs(den), jnp.exp(-m_comb))
        o_ref[:, sl] = (_sigmoid(og_ref[:, sl]) * hid).astype(BF16)

        bl = b_col[L - 1:L, :]
        g = bl - b_col + i_col
        m_new = jnp.maximum(bl + m_prev, jnp.max(g, axis=0, keepdims=True))
        wk = jnp.exp(g - m_new)
        decay = jnp.exp(bl + m_prev - m_new)
        c_sc[h] = decay * c_prev + _dot_tn((wk * vv[:, sl]).astype(BF16), kh)
        n_sc[h] = decay * n_prev + jnp.sum(wk * kc[:, sl], axis=0, keepdims=True)
        m_sc[h] = jnp.broadcast_to(m_new, m_sc.shape[1:])


def _mlstm(z, conv_w, gates_col, gates_row, *, L):
    S = z.shape[0]
    r = L // CONV_HALO
    W = MLSTM_HEADS * MLSTM_HEAD_DIM
    halo = lambda c: jnp.maximum(c * r - 1, 0)
    return pl.pallas_call(
        _mlstm_kernel,
        grid=(S // L,),
        in_specs=[
            pl.BlockSpec((L, W), lambda c: (c, COL_ML_Q // W)),
            pl.BlockSpec((L, W), lambda c: (c, COL_ML_K // W)),
            pl.BlockSpec((L, W), lambda c: (c, COL_ML_V // W)),
            pl.BlockSpec((L, W), lambda c: (c, COL_ML_O // W)),
            pl.BlockSpec((CONV_HALO, W), lambda c: (halo(c), COL_ML_Q // W)),
            pl.BlockSpec((CONV_HALO, W), lambda c: (halo(c), COL_ML_K // W)),
            pl.BlockSpec((MLSTM_CONV, 2 * W), lambda c: (0, 0)),
            pl.BlockSpec((L, N_GATES), lambda c: (c, 0)),
            pl.BlockSpec((N_GATES, L), lambda c: (0, c)),
        ],
        out_specs=pl.BlockSpec((L, W), lambda c: (c, 0)),
        out_shape=jax.ShapeDtypeStruct((S, W), BF16),
        scratch_shapes=[pltpu.VMEM((MLSTM_HEADS, MLSTM_HEAD_DIM, MLSTM_HEAD_DIM), F32),
                        pltpu.VMEM((MLSTM_HEADS, 1, MLSTM_HEAD_DIM), F32),
                        pltpu.VMEM((MLSTM_HEADS, 8, LANES), F32)],
        compiler_params=_cparams(("arbitrary",)),
        name="mlstm",
    )(z, z, z, z, z, z, conv_w, gates_col, gates_row)


def _fox_kernel(q_ref, k_ref, v_ref, cq_ref, ck_ref, o_ref, m_sc, acc_sc, s_sc, *, t):
    i = pl.program_id(1)
    nch = t // LANES
    m_sc[...] = jnp.full_like(m_sc, -jnp.inf)
    acc_sc[...] = jnp.zeros_like(acc_sc)

    q = q_ref[...] * (LOG2E * HEAD_DIM ** -0.5)
    lane = lax.broadcasted_iota(jnp.int32, (t, LANES), 1)
    row = lax.broadcasted_iota(jnp.int32, (t, LANES), 0)
    low = lane < HEAD_DIM
    own = [low, ~low]
    qa = [jnp.where(own[a], q, 0.0).astype(BF16) for a in range(2)]
    col0 = FOX_COL0 + 2 * pl.program_id(0)
    gate_lane = lax.broadcasted_iota(jnp.int32, (t, N_GATES), 1)
    gate_row = lax.broadcasted_iota(jnp.int32, (N_GATES, t), 0)
    gq = cq_ref[...] * LOG2E
    ca = [jnp.broadcast_to(jnp.sum(jnp.where(gate_lane == col0 + a, gq, 0.0), axis=1, keepdims=True),
                           (t, LANES)) for a in range(2)]

    def scores(j, slot):
        j0 = pl.multiple_of(j * t, t)
        kb = k_ref[pl.ds(j0, t), :]
        for a in range(2):
            s_sc[slot, a] = _dot_nt(qa[a], kb)

    def softmax_pv(j, slot, masked):
        j0 = pl.multiple_of(j * t, t)
        vb = v_ref[pl.ds(j0, t), :]
        gk = ck_ref[:, pl.ds(j0, t)] * LOG2E
        ck = jnp.concatenate([jnp.sum(jnp.where(gate_row == col0 + a, gk, 0.0), axis=0, keepdims=True)
                              for a in range(2)], axis=0)
        ps, alphas = [], []
        for a in range(2):
            sc = []
            for c in range(nch):
                x = s_sc[slot, a, :, c * LANES:(c + 1) * LANES] - ck[a:a + 1, c * LANES:(c + 1) * LANES]
                if masked:
                    x = jnp.where(lane + c * LANES <= row, x, -jnp.inf)
                sc.append(x)
            mx = sc[0]
            for c in range(1, nch):
                mx = jnp.maximum(mx, sc[c])
            m_old = m_sc[a]
            m_new = jnp.maximum(m_old, jnp.max(mx, axis=1, keepdims=True) + ca[a])
            alpha = jnp.exp2(m_old - m_new)
            mm = m_new - ca[a]
            ps += [jnp.exp2(x - mm).astype(BF16) for x in sc]
            alphas.append(alpha)
            m_sc[a] = m_new
        one, zero = jnp.ones_like(vb), jnp.zeros_like(vb)
        vblk = jnp.concatenate([jnp.concatenate([jnp.where(low, vb, one), zero], axis=1),
                                jnp.concatenate([zero, jnp.where(low, one, vb)], axis=1)], axis=0)
        acc_sc[...] = jnp.concatenate(alphas, axis=1) * acc_sc[...] + _dot(jnp.concatenate(ps, axis=1), vblk)

    scores(0, 0)

    def pair(j):
        scores(j + 1, 1)
        softmax_pv(j, 0, False)
        scores(j + 2, 0)
        softmax_pv(j + 1, 1, False)

    def body(j4, carry):
        pair(4 * j4)
        pair(4 * j4 + 2)
        return carry

    lax.fori_loop(0, i // 4, body, 0)

    @pl.when(i % 4 >= 2)
    def _():
        pair(4 * (i // 4))

    @pl.when(i % 2 == 1)
    def _():
        scores(i, 1)
        softmax_pv(i - 1, 0, False)
        softmax_pv(i, 1, True)

    @pl.when(i % 2 == 0)
    def _():
        softmax_pv(i, 0, True)

    outs = []
    for a in range(2):
        acc = acc_sc[:, a * LANES:(a + 1) * LANES]
        outs.append(acc / pltpu.roll(acc, HEAD_DIM, axis=1))
    o_ref[...] = jnp.where(low, outs[0], outs[1]).astype(BF16)


def _fox(z, zb, gates_col, gates_row, *, t):
    S = z.shape[0]
    npair = FOX_HEADS // 2
    cq0, ck0, cv0 = COL_FOX_Q // LANES, COL_FOX_K // LANES, COL_FOX_V // LANES
    return pl.pallas_call(
        functools.partial(_fox_kernel, t=t),
        grid=(npair, S // t),
        in_specs=[
            pl.BlockSpec((t, LANES), lambda p, i: (i, cq0 + p)),
            pl.BlockSpec((S, LANES), lambda p, i: (0, ck0 + p)),
            pl.BlockSpec((S, LANES), lambda p, i: (0, cv0 + p)),
            pl.BlockSpec((t, N_GATES), lambda p, i: (i, 0)),
            pl.BlockSpec((N_GATES, S), lambda p, i: (0, 0)),
        ],
        out_specs=pl.BlockSpec((t, LANES), lambda p, i: (i, p)),
        out_shape=jax.ShapeDtypeStruct((S, MIX_WIDTH), BF16),
        scratch_shapes=[pltpu.VMEM((2, t, LANES), F32), pltpu.VMEM((t, 2 * LANES), F32),
                        pltpu.VMEM((2, 2, t, t), F32)],
        compiler_params=_cparams(("parallel", "arbitrary")),
        name="fox",
    )(z, zb, zb, gates_col, gates_row)


def _merge_kernel(hb_ref, ya_ref, yb_ref, yc_ref, yd_ref, wg_ref, wb_ref, o_ref):
    hb = hb_ref[...]
    acc = None
    for b, y_ref in enumerate((ya_ref, yb_ref, yc_ref, yd_ref)):
        gate = _sigmoid(_dot(hb, wg_ref[b]))
        term = gate * _dot(y_ref[...], wb_ref[b])
        acc = term if acc is None else acc + term
    o_ref[...] = acc.astype(BF16)


def _merge(hb, ys, wg, wb, layer, *, tm, tn):
    S, D = hb.shape
    _, nb, W, _ = wb.shape
    yspec = pl.BlockSpec((tm, W), lambda n, i: (i, 0))
    return pl.pallas_call(
        _merge_kernel,
        grid=(D // tn, S // tm),
        in_specs=[pl.BlockSpec((tm, D), lambda n, i: (i, 0)), yspec, yspec, yspec, yspec,
                  pl.BlockSpec((None, nb, D, tn), lambda n, i: (layer, 0, 0, n)),
                  pl.BlockSpec((None, nb, W, tn), lambda n, i: (layer, 0, 0, n))],
        out_specs=pl.BlockSpec((tm, tn), lambda n, i: (i, n)),
        out_shape=jax.ShapeDtypeStruct((S, D), BF16),
        compiler_params=_cparams(("parallel", "arbitrary")),
        name="merge",
    )(hb, *ys, wg, wb)


def _outproj_kernel(m_ref, h_ref, w_ref, g_ref, b_ref, o_ref, *, alpha):
    g, b = g_ref[...], b_ref[...]
    rows = 128
    for r in range(m_ref.shape[0] // rows):
        rs = slice(r * rows, (r + 1) * rows)
        y = alpha * h_ref[rs, :] + _dot(m_ref[rs, :], w_ref[...])
        o_ref[rs, :] = _layer_norm(y, g, b)


def _outproj(merged, h, w, layer, g, b, *, alpha, tm):
    S, D = h.shape
    return pl.pallas_call(
        functools.partial(_outproj_kernel, alpha=alpha),
        grid=(S // tm,),
        in_specs=[pl.BlockSpec((tm, D), lambda i: (i, 0)), pl.BlockSpec((tm, D), lambda i: (i, 0)),
                  pl.BlockSpec((None, D, D), lambda i: (layer, 0, 0)),
                  pl.BlockSpec((1, D), lambda i: (0, 0)), pl.BlockSpec((1, D), lambda i: (0, 0))],
        out_specs=pl.BlockSpec((tm, D), lambda i: (i, 0)),
        out_shape=jax.ShapeDtypeStruct((S, D), F32),
        compiler_params=_cparams(("parallel",)),
        name="outproj",
    )(merged, h, w, g.reshape(1, D), b.reshape(1, D))


def _rope_tables(S):
    half = HEAD_DIM // 2
    inv = ROPE_THETA ** (-jnp.arange(half, dtype=F32) / half)
    blk = SWA_BLOCK
    base = (jnp.arange(S // blk, dtype=jnp.int32) * blk).astype(F32)[:, None] * inv[None, :]
    off = jnp.arange(blk, dtype=jnp.int32).astype(F32)[:, None] * inv[None, :]
    cb, sb = jnp.cos(base)[:, None, :], jnp.sin(base)[:, None, :]
    co, so = jnp.cos(off)[None], jnp.sin(off)[None]
    cos = (cb * co - sb * so).reshape(S, half)
    sin = (sb * co + cb * so).reshape(S, half)
    cos_t = jnp.concatenate([cos, cos, cos, cos], axis=1)
    sin_t = jnp.concatenate([-sin, -sin, sin, sin], axis=1)
    return cos_t, sin_t


def _pick(n, prefs):
    for p in prefs:
        if n % p == 0:
            return p
    return n


def kernel(x, ln_g, ln_b, ffn1_w_in, ffn1_w_out, w_in, pool_w, pool_scale, swa_sinks, mlstm_conv, mlstm_i_bias, mlstm_f_bias, fox_f_bias, w_branch, w_gate, w_out, ffn2_w_in, ffn2_w_out):
    B, S, D = x.shape
    depth = ln_g.shape[0]
    assert B == 1 and w_in.shape[2] == 4880 and S % ML_CHUNK == 0
    alpha = float((2 * depth) ** 0.25)
    cos_t, sin_t = _rope_tables(S)
    tm = _pick(S, (512, 256))
    tm_ffn = _pick(S, (1024, 512, 256))
    t_mix = _pick(S, (512, 256))
    tq = _pick(S, (512, 256))

    g = SWA_HEADS // SWA_KV_HEADS
    wi = w_in.astype(BF16)
    hh = HEAD_DIM // 2
    swa_q = (wi[..., 512:1024].reshape(depth, D, SWA_KV_HEADS, g, 2, hh)
             .transpose(0, 1, 3, 4, 2, 5).reshape(depth, D, MIX_WIDTH))
    swa_k = (wi[..., 1024:1152].reshape(depth, D, SWA_KV_HEADS, 2, hh)
             .swapaxes(2, 3).reshape(depth, D, SWA_KV_HEADS * HEAD_DIM))
    wz = jnp.concatenate([
        wi[..., 0:512], swa_q, wi[..., 1280:3328], wi[..., 3336:4872],
        swa_k, wi[..., 1152:1280], wi[..., 3328:3336], wi[..., 4872:4880],
        jnp.zeros((depth, D, Z_COLS - COL_GATES - N_GATES), BF16)], axis=2)
    wbr = w_branch.astype(BF16)
    swa_rows = wbr[:, 1].reshape(depth, SWA_KV_HEADS, g, HEAD_DIM, D).swapaxes(1, 2).reshape(depth, 1, MIX_WIDTH, D)
    wb = jnp.concatenate([wbr[:, :1], swa_rows, wbr[:, 2:]], axis=1)
    wg = w_gate.astype(BF16)
    wo = w_out.astype(BF16)
    f1_in, f1_out = ffn1_w_in.astype(BF16), ffn1_w_out.astype(BF16)
    f2_in, f2_out = ffn2_w_in.astype(BF16), ffn2_w_out.astype(BF16)

    h = x.reshape(S, D)
    for l in range(depth):
        gate_bias = jnp.concatenate([mlstm_i_bias[l], mlstm_f_bias[l], fox_f_bias[l]]).reshape(1, N_GATES)
        sinks = swa_sinks[l]

        h, hb = _ffn(h, f1_in, f1_out, l, ln_g[l, 0], ln_b[l, 0], alpha=alpha, tm=tm_ffn, tf=512)
        z, zb = _inproj(hb, wz, l, tm=tm, tn=Z_COLS // 2)
        gates_col = _gates(z, gate_bias, T=ML_CHUNK)
        gates_row = gates_col.T

        ya = _pool(z, pool_w[l].astype(BF16), pool_scale[l], T=t_mix)
        yb = _swa(z, sinks, cos_t, sin_t, T=t_mix)
        yc = _mlstm(z, mlstm_conv[l], gates_col, gates_row, L=ML_CHUNK)
        yd = _fox(z, zb, gates_col, gates_row, t=tq)
        merged = _merge(hb, (ya, yb, yc, yd), wg, wb, l, tm=_pick(S, (1024, 512, 256)), tn=512)
        h = _outproj(merged, h, wo, l, ln_g[l, 1], ln_b[l, 1], alpha=alpha, tm=tm)
        h, hb = _ffn(h, f2_in, f2_out, l, ln_g[l, 2], ln_b[l, 2], alpha=alpha, tm=tm_ffn, tf=512)
    return h.reshape(B, S, D)
```
